```python
import jax, jax.numpy as jnp
from jax import lax
import numpy as np

D_MODEL = 1024
BATCH = 4
SEQ = 4096
DEPTH = 1
DEC_BATCH = 128
DEC_SEQ = 8
PAST_LEN = 16384
PAGE_SIZE = 128

N_HEADS = 8
QK_NOPE = 64
ROPE_DIM = 32
V_DIM = 64
Q_RANK = 384
KV_RANK = 256
CONV_DIM = 512
CONV_WIDTH = 31
D_ATTN = N_HEADS * V_DIM
D_MIX = D_ATTN + CONV_DIM
IN_COLS = Q_RANK + KV_RANK + ROPE_DIM + 2 * CONV_DIM
D_FF = 2816
ROPE_THETA = 10000.0
Q_BLOCK = 128
EPS = 1e-6
SM_SCALE = (QK_NOPE + ROPE_DIM) ** -0.5
NEG_INF = -1e30

kernel_name = "mla_conformer_hybrid_step"


def rms_norm(x, g):
    xf = x.astype(jnp.float32)
    y = xf * lax.rsqrt(jnp.mean(xf * xf, axis=-1, keepdims=True) + EPS)
    return (y * g.astype(jnp.float32)).astype(x.dtype)


def layer_norm(x, g, b):
    xf = x.astype(jnp.float32)
    mu = jnp.mean(xf, axis=-1, keepdims=True)
    var = jnp.mean(jnp.square(xf - mu), axis=-1, keepdims=True)
    y = (xf - mu) * lax.rsqrt(var + EPS)
    return (y * g.astype(jnp.float32) + b.astype(jnp.float32)).astype(x.dtype)


def swiglu(x, wg, wu, wd):
    return (jax.nn.silu(x @ wg) * (x @ wu)) @ wd


def rope_tables(pos, dtype):
    inv_freq = ROPE_THETA ** (-jnp.arange(0, ROPE_DIM, 2, dtype=jnp.float32) / ROPE_DIM)
    ang = pos.astype(jnp.float32)[:, None] * inv_freq[None, :]
    return jnp.cos(ang).astype(dtype), jnp.sin(ang).astype(dtype)


def apply_rope(x, cos, sin):
    x1, x2 = jnp.split(x, 2, axis=-1)
    return jnp.concatenate([x1 * cos - x2 * sin, x1 * sin + x2 * cos], axis=-1)


def attend_prompt(q_abs, q_pe, c_kv, k_pe):
    B, S, H, C = q_abs.shape
    nb = S // Q_BLOCK
    qa = q_abs.reshape(B, nb, Q_BLOCK, H, C).swapaxes(0, 1)
    qp = q_pe.reshape(B, nb, Q_BLOCK, H, ROPE_DIM).swapaxes(0, 1)
    kpos = jnp.arange(S)

    def one_block(args):
        blk, qa_b, qp_b = args
        s = (jnp.einsum('bthc,bsc->bhts', qa_b, c_kv)
             + jnp.einsum('bthr,bsr->bhts', qp_b, k_pe)).astype(jnp.float32) * SM_SCALE
        qpos = blk * Q_BLOCK + jnp.arange(Q_BLOCK)
        s = jnp.where(qpos[:, None] >= kpos[None, :], s, NEG_INF)
        p = jax.nn.softmax(s, axis=-1).astype(c_kv.dtype)
        return jnp.einsum('bhts,bsc->bthc', p, c_kv)

    o = lax.map(one_block, (jnp.arange(nb), qa, qp))
    return o.swapaxes(0, 1).reshape(B, S, H, C)


def attend_sample(q_abs, q_pe, c_kv, k_pe, cache_kv, cache_kr, page_table):
    DB, T = q_abs.shape[0], q_abs.shape[1]
    c_past = cache_kv[page_table].reshape(DB, -1, KV_RANK)
    r_past = cache_kr[page_table].reshape(DB, -1, ROPE_DIM)
    P = c_past.shape[1]
    s_past = jnp.einsum('bthc,bsc->bhts', q_abs, c_past) + jnp.einsum('bthr,bsr->bhts', q_pe, r_past)
    s_new = jnp.einsum('bthc,bsc->bhts', q_abs, c_kv) + jnp.einsum('bthr,bsr->bhts', q_pe, k_pe)
    causal = jnp.tril(jnp.ones((T, T), dtype=bool))
    s_new = jnp.where(causal, s_new.astype(jnp.float32) * SM_SCALE, NEG_INF)
    s = jnp.concatenate([s_past.astype(jnp.float32) * SM_SCALE, s_new], axis=-1)
    p = jax.nn.softmax(s, axis=-1).astype(c_kv.dtype)
    return (jnp.einsum('bhts,bsc->bthc', p[..., :P], c_past)
            + jnp.einsum('bhts,bsc->bthc', p[..., P:], c_kv))


def hybrid_layer(x, pos, conv_state, attend, lp):
    x = x + 0.5 * swiglu(rms_norm(x, lp['g_ffn1']), lp['w1_gate'], lp['w1_up'], lp['w1_down'])
    h = rms_norm(x, lp['g_mix'])
    u = h @ lp['w_in']
    o1 = Q_RANK
    o2 = o1 + KV_RANK
    o3 = o2 + ROPE_DIM
    q_c, kv_c, kr, conv_in = u[..., :o1], u[..., o1:o2], u[..., o2:o3], u[..., o3:]
    cos, sin = rope_tables(pos, x.dtype)
    q = jnp.einsum('btq,qhd->bthd', rms_norm(q_c, lp['g_q']), lp['w_q_b'])
    q_nope, q_pe = q[..., :QK_NOPE], apply_rope(q[..., QK_NOPE:], cos[:, None, :], sin[:, None, :])
    c_kv = rms_norm(kv_c, lp['g_kv'])
    k_pe = apply_rope(kr, cos, sin)
    w_uk = lp['w_kv_b'][..., :QK_NOPE]
    w_uv = lp['w_kv_b'][..., QK_NOPE:]
    q_abs = jnp.einsum('bthn,chn->bthc', q_nope, w_uk)
    o_lat = attend(q_abs, q_pe, c_kv, k_pe)
    attn_out = jnp.einsum('bthc,chv->bthv', o_lat, w_uv).reshape(x.shape[0], x.shape[1], D_ATTN)
    ga, gb = conv_in[..., :CONV_DIM], conv_in[..., CONV_DIM:]
    g = ga * jax.nn.sigmoid(gb)
    ext = jnp.concatenate([conv_state, g], axis=1)
    y_dw = lax.conv_general_dilated(
        ext, lp['w_dw'][:, None, :], window_strides=(1,), padding='VALID',
        dimension_numbers=('NWC', 'WIO', 'NWC'), feature_group_count=CONV_DIM) + lp['b_dw']
    conv_out = jax.nn.silu(layer_norm(y_dw, lp['g_cn'], lp['b_cn']))
    new_conv = ext[:, -(CONV_WIDTH - 1):]
    mix = jnp.concatenate([rms_norm(attn_out, lp['g_out_attn']), rms_norm(conv_out, lp['g_out_conv'])], axis=-1)
    x = x + mix @ lp['w_out']
    x = x + 0.5 * swiglu(rms_norm(x, lp['g_ffn2']), lp['w2_gate'], lp['w2_up'], lp['w2_down'])
    return x, c_kv, k_pe, new_conv


def setup_inputs(seed: int = 0) -> dict:
    key = jax.random.key(seed)
    ks = jax.random.split(key, 40)
    f32 = jnp.float32
    n_pages = PAST_LEN // PAGE_SIZE
    n_used = DEC_BATCH * n_pages
    n_phys = (n_used * 5) // 4

    def nrm(k, shape, scale):
        return jax.random.normal(k, shape, f32) * scale

    def gain(k, shape):
        return 1.0 + 0.01 * jax.random.normal(k, shape, f32)

    L = DEPTH
    perm = jax.random.permutation(ks[5], n_phys)[:n_used]
    return {
        'x_prompt': nrm(ks[0], (BATCH, SEQ, D_MODEL), 1.0),
        'x_sample': nrm(ks[1], (DEC_BATCH, DEC_SEQ, D_MODEL), 1.0),
        'cache_kv_latent': nrm(ks[2], (L, n_phys, PAGE_SIZE, KV_RANK), 1.0),
        'cache_k_rope': nrm(ks[3], (L, n_phys, PAGE_SIZE, ROPE_DIM), 1.0),
        'state_conv': nrm(ks[4], (L, DEC_BATCH, CONV_WIDTH - 1, CONV_DIM), 0.5),
        'page_table': perm.reshape(DEC_BATCH, n_pages).astype(jnp.int32),
        'g_ffn1': gain(ks[6], (L, D_MODEL)),
        'w1_gate': nrm(ks[7], (L, D_MODEL, D_FF), D_MODEL ** -0.5),
        'w1_up': nrm(ks[8], (L, D_MODEL, D_FF), D_MODEL ** -0.5),
        'w1_down': nrm(ks[9], (L, D_FF, D_MODEL), D_FF ** -0.5),
        'g_mix': gain(ks[10], (L, D_MODEL)),
        'w_in': nrm(ks[11], (L, D_MODEL, IN_COLS), D_MODEL ** -0.5),
        'g_q': gain(ks[12], (L, Q_RANK)),
        'w_q_b': nrm(ks[13], (L, Q_RANK, N_HEADS, QK_NOPE + ROPE_DIM), Q_RANK ** -0.5),
        'g_kv': gain(ks[14], (L, KV_RANK)),
        'w_kv_b': nrm(ks[15], (L, KV_RANK, N_HEADS, QK_NOPE + V_DIM), KV_RANK ** -0.5),
        'w_dw': nrm(ks[16], (L, CONV_WIDTH, CONV_DIM), CONV_WIDTH ** -0.5),
        'b_dw': nrm(ks[17], (L, CONV_DIM), 0.01),
        'g_cn': gain(ks[18], (L, CONV_DIM)),
        'b_cn': nrm(ks[19], (L, CONV_DIM), 0.01),
        'g_out_attn': gain(ks[20], (L, D_ATTN)),
        'g_out_conv': gain(ks[21], (L, CONV_DIM)),
        'w_out': nrm(ks[22], (L, D_MIX, D_MODEL), D_MIX ** -0.5),
        'g_ffn2': gain(ks[23], (L, D_MODEL)),
        'w2_gate': nrm(ks[24], (L, D_MODEL, D_FF), D_MODEL ** -0.5),
        'w2_up': nrm(ks[25], (L, D_MODEL, D_FF), D_MODEL ** -0.5),
        'w2_down': nrm(ks[26], (L, D_FF, D_MODEL), D_FF ** -0.5),
        'g_final': gain(ks[27], (D_MODEL,)),
    }


def reference(x_prompt, x_sample, cache_kv_latent, cache_k_rope, state_conv, page_table,
              g_ffn1, w1_gate, w1_up, w1_down, g_mix, w_in, g_q, w_q_b, g_kv, w_kv_b,
              w_dw, b_dw, g_cn, b_cn, g_out_attn, g_out_conv, w_out,
              g_ffn2, w2_gate, w2_up, w2_down, g_final):
    B, S = x_prompt.shape[0], x_prompt.shape[1]
    DB, T = x_sample.shape[0], x_sample.shape[1]
    past_len = page_table.shape[1] * PAGE_SIZE
    pos_prompt = jnp.arange(S)
    pos_sample = past_len + jnp.arange(T)

    xp, xs = x_prompt, x_sample
    kv_p, kr_p, cv_p, kv_s, kr_s, cv_s = [], [], [], [], [], []
    for l in range(DEPTH):
        lp = dict(g_ffn1=g_ffn1[l], w1_gate=w1_gate[l], w1_up=w1_up[l], w1_down=w1_down[l],
                  g_mix=g_mix[l], w_in=w_in[l], g_q=g_q[l], w_q_b=w_q_b[l], g_kv=g_kv[l],
                  w_kv_b=w_kv_b[l], w_dw=w_dw[l], b_dw=b_dw[l], g_cn=g_cn[l], b_cn=b_cn[l],
                  g_out_attn=g_out_attn[l], g_out_conv=g_out_conv[l], w_out=w_out[l],
                  g_ffn2=g_ffn2[l], w2_gate=w2_gate[l], w2_up=w2_up[l], w2_down=w2_down[l])
        zero_state = jnp.zeros((B, CONV_WIDTH - 1, CONV_DIM), dtype=xp.dtype)
        xp, ckv, kpe, ncv = hybrid_layer(xp, pos_prompt, zero_state, attend_prompt, lp)
        kv_p.append(ckv.reshape(B, S // PAGE_SIZE, PAGE_SIZE, KV_RANK))
        kr_p.append(kpe.reshape(B, S // PAGE_SIZE, PAGE_SIZE, ROPE_DIM))
        cv_p.append(ncv)

        ck_l, cr_l = cache_kv_latent[l], cache_k_rope[l]

        def attend_s(q_abs, q_pe, c_kv, k_pe, ck_l=ck_l, cr_l=cr_l):
            return attend_sample(q_abs, q_pe, c_kv, k_pe, ck_l, cr_l, page_table)

        xs, ckv, kpe, ncv = hybrid_layer(xs, pos_sample, state_conv[l], attend_s, lp)
        kv_s.append(ckv)
        kr_s.append(kpe)
        cv_s.append(ncv)

    y_prompt = rms_norm(xp, g_final)
    y_sample = rms_norm(xs, g_final)
    return (y_prompt, y_sample, jnp.stack(kv_p), jnp.stack(kr_p), jnp.stack(cv_p),
            jnp.stack(kv_s), jnp.stack(kr_s), jnp.stack(cv_s))
```

```python
import functools

import jax
import jax.numpy as jnp
from jax import lax
from jax.experimental import pallas as pl
from jax.experimental.pallas import tpu as pltpu

D_MODEL = 1024
N_HEADS = 8
QK_NOPE = 64
ROPE_DIM = 32
HALF_ROPE = ROPE_DIM // 2
V_DIM = 64
Q_RANK = 384
KV_RANK = 256
CONV_DIM = 512
CONV_WIDTH = 31
CONV_STATE = CONV_WIDTH - 1
D_ATTN = N_HEADS * V_DIM
D_FF = 2816
PAGE_SIZE = 128
ROPE_THETA = 10000.0
EPS = 1e-6
SM_SCALE = (QK_NOPE + ROPE_DIM) ** -0.5
NEG_INF = -1e30

LANES = 128
SUBLANES = 8
HEAD_PAD = LANES
D_HEADS_PAD = N_HEADS * HEAD_PAD
KR_OFF = Q_RANK + KV_RANK
GA_OFF = KR_OFF + LANES
GB_OFF = GA_OFF + CONV_DIM
IN_COLS_PAD = GB_OFF + CONV_DIM
VMEM_LIMIT = 56 * 1024 * 1024

TOKEN_TILE = 512
ATTN_TILE = 512
CONV_TILE = 512
PAGES_PER_STEP = 16

BF16 = jnp.bfloat16
F32 = jnp.float32


def _dot(a, b):
    return jnp.dot(a, b, preferred_element_type=F32)


def _dot_nt(a, b):
    return lax.dot_general(a, b, (((1,), (1,)), ((), ())), preferred_element_type=F32)


def _rms(x, g, n=None):
    n = x.shape[-1] if n is None else n
    ms = jnp.sum(x * x, axis=-1, keepdims=True) * (1.0 / n)
    return x * lax.rsqrt(ms + EPS) * g


def _rope_tile(blk, c, s1, s2):
    return (blk * c + pltpu.roll(blk, HALF_ROPE, 1) * s1
            + pltpu.roll(blk, LANES - HALF_ROPE, 1) * s2)


def _const_spec(shape):
    nd = len(shape)
    return pl.BlockSpec(shape, lambda *_: (0,) * nd, pipeline_mode=pl.Buffered(1))


def _params(sem):
    return pltpu.CompilerParams(dimension_semantics=sem, vmem_limit_bytes=VMEM_LIMIT)


def _ffn_kernel(x_ref, g_ref, wg_ref, wu_ref, wd_ref, gf_ref, o_ref, *, final_norm):
    x = x_ref[...]
    h = _rms(x, g_ref[...]).astype(BF16)
    a = _dot(h, wg_ref[...])
    a = a * jax.nn.sigmoid(a) * _dot(h, wu_ref[...])
    y = x + 0.5 * _dot(a.astype(BF16), wd_ref[...])
    if final_norm:
        y = _rms(y, gf_ref[...])
    o_ref[...] = y


def _ffn(x, g, wg, wu, wd, gf, final_norm):
    n = x.shape[0]
    tm = TOKEN_TILE
    tok = pl.BlockSpec((tm, D_MODEL), lambda i: (i, 0))
    return pl.pallas_call(
        functools.partial(_ffn_kernel, final_norm=final_norm),
        out_shape=jax.ShapeDtypeStruct((n, D_MODEL), F32),
        grid=(n // tm,),
        in_specs=[tok, _const_spec((1, D_MODEL)), _const_spec((D_MODEL, D_FF)),
                  _const_spec((D_MODEL, D_FF)), _const_spec((D_FF, D_MODEL)),
                  _const_spec((1, D_MODEL))],
        out_specs=tok,
        compiler_params=_params(("parallel",)),
        name="ffn_final" if final_norm else "ffn",
    )(x, g, wg, wu, wd, gf)


def _inproj_kernel(x_ref, gmix_ref, win_ref, gq_ref, wq_ref, gkv_ref, wa_ref, wb_ref,
                   c_ref, s1_ref, s2_ref,
                   q_ref, a_ref, b_ref, ckv_ref, kpe_ref, glu_ref, *, absorbed):
    x = x_ref[...]
    h = _rms(x, gmix_ref[...]).astype(BF16)
    u = _dot(h, win_ref[...])
    c, s1, s2 = c_ref[...], s1_ref[...], s2_ref[...]

    qn = _rms(u[:, :Q_RANK], gq_ref[...]).astype(BF16)
    q_raw = _dot(qn, wq_ref[...])
    ckv = _rms(u[:, Q_RANK:KR_OFF], gkv_ref[...])
    ckv_ref[...] = ckv
    ckv_b = ckv.astype(BF16)
    kpe = _rope_tile(u[:, KR_OFF:GA_OFF], c, s1, s2)
    kpe_ref[...] = kpe[:, :ROPE_DIM]
    ga = u[:, GA_OFF:GB_OFF]
    gb = u[:, GB_OFF:IN_COLS_PAD]
    glu_ref[...] = ga * jax.nn.sigmoid(gb)

    if not absorbed:
        k_nope = _dot(ckv_b, wa_ref[...])
        b_ref[...] = _dot(ckv_b, wb_ref[...]).astype(BF16)
    for hd in range(N_HEADS):
        sl = slice(hd * HEAD_PAD, (hd + 1) * HEAD_PAD)
        qh = (_rope_tile(q_raw[:, sl], c, s1, s2) * SM_SCALE).astype(BF16)
        q_ref[:, sl] = qh
        if absorbed:
            a_ref[:, hd * KV_RANK:(hd + 1) * KV_RANK] = _dot(qh, wa_ref[hd]).astype(BF16)
        else:
            a_ref[:, sl] = (k_nope[:, sl] + kpe).astype(BF16)
    if absorbed:
        b_ref[...] = jnp.zeros_like(b_ref)


def _inproj(x, gmix, win, gq, wq, gkv, wa, wb, c, s1, s2, absorbed):
    n = x.shape[0]
    tm = TOKEN_TILE
    row = lambda w: pl.BlockSpec((tm, w), lambda i: (i, 0))
    a_w = N_HEADS * KV_RANK if absorbed else D_HEADS_PAD
    b_w = LANES if absorbed else D_HEADS_PAD
    out_shape = (
        jax.ShapeDtypeStruct((n, D_HEADS_PAD), BF16),
        jax.ShapeDtypeStruct((n, a_w), BF16),
        jax.ShapeDtypeStruct((n, b_w), BF16),
        jax.ShapeDtypeStruct((n, KV_RANK), F32),
        jax.ShapeDtypeStruct((n, ROPE_DIM), F32),
        jax.ShapeDtypeStruct((n, CONV_DIM), F32),
    )
    return pl.pallas_call(
        functools.partial(_inproj_kernel, absorbed=absorbed),
        out_shape=out_shape,
        grid=(n // tm,),
        in_specs=[row(D_MODEL), _const_spec((1, D_MODEL)), _const_spec(win.shape),
                  _const_spec((1, Q_RANK)), _const_spec(wq.shape), _const_spec((1, KV_RANK)),
                  _const_spec(wa.shape), _const_spec(wb.shape),
                  row(LANES), row(LANES), row(LANES)],
        out_specs=[row(D_HEADS_PAD), row(a_w), row(b_w), row(KV_RANK), row(ROPE_DIM),
                   row(CONV_DIM)],
        compiler_params=_params(("parallel",)),
        name="inproj_absorbed" if absorbed else "inproj",
    )(x, gmix, win, gq, wq, gkv, wa, wb, c, s1, s2)


def _flash_kernel(q_ref, k_ref, v_ref, o_ref):
    t = ATTN_TILE
    qi = pl.program_id(2)
    q = q_ref[...]

    def tile(j, carry, masked):
        m, l, acc = carry
        start = pl.multiple_of(j * t, t)
        k = k_ref[pl.ds(start, t), :]
        v = v_ref[pl.ds(start, t), :]
        s = _dot_nt(q, k)
        if masked:
            rows = lax.broadcasted_iota(jnp.int32, (t, t), 0)
            cols = lax.broadcasted_iota(jnp.int32, (t, t), 1)
            s = jnp.where(rows >= cols, s, NEG_INF)
        m_new = jnp.maximum(m, jnp.max(s, axis=-1, keepdims=True))
        alpha = jnp.exp(m - m_new)
        p = jnp.exp(s - m_new)
        l = alpha * l + jnp.sum(p, axis=-1, keepdims=True)
        acc = alpha * acc + _dot(p.astype(BF16), v)
        return m_new, l, acc

    init = (jnp.full((t, 1), NEG_INF, F32), jnp.zeros((t, 1), F32), jnp.zeros((t, HEAD_PAD), F32))
    carry = lax.fori_loop(0, qi, lambda j, cr: tile(j, cr, False), init)
    m, l, acc = tile(qi, carry, True)
    o_ref[...] = acc / l


def _flash(q, k, v, batch, seq):
    t = ATTN_TILE
    nq = seq // t
    q_spec = pl.BlockSpec((t, HEAD_PAD), lambda b, h, i: (b * nq + i, h))
    kv_spec = pl.BlockSpec((seq, HEAD_PAD), lambda b, h, i: (b, h))
    return pl.pallas_call(
        _flash_kernel,
        out_shape=jax.ShapeDtypeStruct((batch * seq, D_HEADS_PAD), F32),
        grid=(batch, N_HEADS, nq),
        in_specs=[q_spec, kv_spec, kv_spec],
        out_specs=q_spec,
        compiler_params=_params(("parallel", "parallel", "arbitrary")),
        name="flash_prompt",
    )(q, k, v)


def _paged_kernel(pt_ref, qa_ref, qp_ref, cn_ref, kn_ref, *rest, n_new):
    p_cnt = PAGES_PER_STEP
    kv_refs = rest[:p_cnt]
    kr_refs = rest[p_cnt:2 * p_cnt]
    o_ref = rest[2 * p_cnt]
    kbuf, rbuf, m_ref, l_ref, acc_ref = rest[2 * p_cnt + 1:]
    c = pl.program_id(1)
    rows = qa_ref.shape[1]
    qa = qa_ref[0]
    qp = qp_ref[0][:, :ROPE_DIM]

    @pl.when(c == 0)
    def _():
        cn = cn_ref[0].astype(BF16)
        kn = kn_ref[0].astype(BF16)
        s = _dot_nt(qa, cn) + _dot_nt(qp, kn)
        tq = lax.broadcasted_iota(jnp.int32, (rows, n_new), 0) // N_HEADS
        tk = lax.broadcasted_iota(jnp.int32, (rows, n_new), 1)
        s = jnp.where(tq >= tk, s, NEG_INF)
        m = jnp.max(s, axis=-1, keepdims=True)
        p = jnp.exp(s - m)
        m_ref[...] = m
        l_ref[...] = jnp.sum(p, axis=-1, keepdims=True)
        acc_ref[...] = _dot(p.astype(BF16), cn)

    for j in range(p_cnt):
        kbuf[j * PAGE_SIZE:(j + 1) * PAGE_SIZE, :] = kv_refs[j][...].astype(BF16)
        rbuf[j * PAGE_SIZE:(j + 1) * PAGE_SIZE, :] = kr_refs[j][...].astype(BF16)
    kb = kbuf[...]
    s = _dot_nt(qa, kb) + _dot_nt(qp, rbuf[...])
    m_old = m_ref[...]
    m_new = jnp.maximum(m_old, jnp.max(s, axis=-1, keepdims=True))
    alpha = jnp.exp(m_old - m_new)
    p = jnp.exp(s - m_new)
    l_new = alpha * l_ref[...] + jnp.sum(p, axis=-1, keepdims=True)
    acc_new = alpha * acc_ref[...] + _dot(p.astype(BF16), kb)
    m_ref[...] = m_new
    l_ref[...] = l_new
    acc_ref[...] = acc_new

    @pl.when(c == pl.num_programs(1) - 1)
    def _():
        o_ref[0] = acc_new / l_new


def _paged_attention(page_table, q_abs, q_rope, c_new, k_new, cache_kv, cache_kr):
    db, rows = q_abs.shape[0], q_abs.shape[1]
    n_new = c_new.shape[1]
    n_pages = page_table.shape[1]
    p_cnt = PAGES_PER_STEP
    n_chunks = n_pages // p_cnt
    pt_flat = page_table.reshape(-1)

    def page_spec(width, j):
        return pl.BlockSpec(
            (None, None, PAGE_SIZE, width),
            lambda b, c, pt, j=j: (0, pt[b * n_pages + c * p_cnt + j], 0, 0))

    per_b = lambda r, w: pl.BlockSpec((1, r, w), lambda b, c, pt: (b, 0, 0))
    grid_spec = pltpu.PrefetchScalarGridSpec(
        num_scalar_prefetch=1,
        grid=(db, n_chunks),
        in_specs=[per_b(rows, KV_RANK), per_b(rows, HEAD_PAD), per_b(n_new, KV_RANK),
                  per_b(n_new, ROPE_DIM)]
                 + [page_spec(KV_RANK, j) for j in range(p_cnt)]
                 + [page_spec(ROPE_DIM, j) for j in range(p_cnt)],
        out_specs=per_b(rows, KV_RANK),
        scratch_shapes=[pltpu.VMEM((p_cnt * PAGE_SIZE, KV_RANK), BF16),
                        pltpu.VMEM((p_cnt * PAGE_SIZE, ROPE_DIM), BF16),
                        pltpu.VMEM((rows, 1), F32), pltpu.VMEM((rows, 1), F32),
                        pltpu.VMEM((rows, KV_RANK), F32)],
    )
    return pl.pallas_call(
        functools.partial(_paged_kernel, n_new=n_new),
        out_shape=jax.ShapeDtypeStruct((db, rows, KV_RANK), F32),
        grid_spec=grid_spec,
        compiler_params=_params(("parallel", "arbitrary")),
        name="paged_attention",
    )(pt_flat, q_abs, q_rope, c_new, k_new, *([cache_kv] * p_cnt), *([cache_kr] * p_cnt))


EXT_HEAD = 32


def _conv_kernel(g_ref, st_ref, w_ref, bdw_ref, gcn_ref, bcn_ref, gout_ref, o_ref, ext_ref, *, tt):
    ti = pl.program_id(1)
    pad = EXT_HEAD - CONV_STATE

    @pl.when(ti == 0)
    def _():
        ext_ref[pad:EXT_HEAD, :] = st_ref[0]

    @pl.when(ti > 0)
    def _():
        ext_ref[0:EXT_HEAD, :] = ext_ref[tt:tt + EXT_HEAD, :]

    ext_ref[EXT_HEAD:EXT_HEAD + tt, :] = g_ref[0]
    acc = jnp.zeros((tt, CONV_DIM), F32) + bdw_ref[...]
    for w in range(CONV_WIDTH):
        acc = acc + ext_ref[pad + w:pad + w + tt, :] * w_ref[w:w + 1, :]
    mu = jnp.mean(acc, axis=-1, keepdims=True)
    d = acc - mu
    var = jnp.mean(d * d, axis=-1, keepdims=True)
    y = d * lax.rsqrt(var + EPS) * gcn_ref[...] + bcn_ref[...]
    y = y * jax.nn.sigmoid(y)
    o_ref[0] = _rms(y, gout_ref[...])


def _conv(glu, state, w_dw, b_dw, g_cn, b_cn, g_out, tt):
    b, t = glu.shape[0], glu.shape[1]
    vec = _const_spec((1, CONV_DIM))
    return pl.pallas_call(
        functools.partial(_conv_kernel, tt=tt),
        out_shape=jax.ShapeDtypeStruct((b, t, CONV_DIM), F32),
        grid=(b, t // tt),
        in_specs=[pl.BlockSpec((1, tt, CONV_DIM), lambda i, j: (i, j, 0)),
                  pl.BlockSpec((1, CONV_STATE, CONV_DIM), lambda i, j: (i, 0, 0)),
                  _const_spec(w_dw.shape), vec, vec, vec, vec],
        out_specs=pl.BlockSpec((1, tt, CONV_DIM), lambda i, j: (i, j, 0)),
        scratch_shapes=[pltpu.VMEM((EXT_HEAD + tt, CONV_DIM), F32)],
        compiler_params=_params(("parallel", "arbitrary")),
        name="conv_module",
    )(glu, state, w_dw, b_dw, g_cn, b_cn, g_out)


def _outproj_kernel(x_ref, a_ref, cv_ref, wuv_ref, ga_ref, woa_ref, woc_ref, o_ref, *, absorbed):
    if absorbed:
        attn = _dot(a_ref[...].astype(BF16), wuv_ref[...])
    else:
        attn = a_ref[...]
    an = _rms(attn, ga_ref[...], n=D_ATTN).astype(BF16)
    o_ref[...] = (x_ref[...] + _dot(an, woa_ref[...])
                  + _dot(cv_ref[...].astype(BF16), woc_ref[...]))


def _outproj(x, attn, conv, wuv, ga, woa, woc, absorbed):
    n = x.shape[0]
    tm = TOKEN_TILE
    row = lambda w: pl.BlockSpec((tm, w), lambda i: (i, 0))
    return pl.pallas_call(
        functools.partial(_outproj_kernel, absorbed=absorbed),
        out_shape=jax.ShapeDtypeStruct((n, D_MODEL), F32),
        grid=(n // tm,),
        in_specs=[row(D_MODEL), row(attn.shape[1]), row(CONV_DIM), _const_spec(wuv.shape),
                  _const_spec((1, D_HEADS_PAD)), _const_spec(woa.shape), _const_spec(woc.shape)],
        out_specs=row(D_MODEL),
        compiler_params=_params(("parallel",)),
        name="outproj_absorbed" if absorbed else "outproj",
    )(x, attn, conv, wuv, ga, woa, woc)


def _pad_heads(w, lo):
    d = w.shape[-1]
    w = jnp.pad(w, [(0, 0)] * (w.ndim - 1) + [(lo, HEAD_PAD - lo - d)])
    return w.reshape(*w.shape[:-2], D_HEADS_PAD)


def _rope_tables(pos):
    inv_freq = ROPE_THETA ** (-jnp.arange(0, ROPE_DIM, 2, dtype=F32) / ROPE_DIM)
    ang = pos.astype(F32)[:, None] * inv_freq[None, :]
    cos, sin = jnp.cos(ang), jnp.sin(ang)
    n = pos.shape[0]
    zeros = jnp.zeros((n, LANES - ROPE_DIM), F32)
    half0 = jnp.zeros((n, HALF_ROPE), F32)
    c = jnp.concatenate([cos, cos, jnp.ones((n, LANES - ROPE_DIM), F32)], axis=1)
    s1 = jnp.concatenate([half0, sin, zeros], axis=1)
    s2 = jnp.concatenate([-sin, half0, zeros], axis=1)
    return c, s1, s2


def kernel(x_prompt, x_sample, cache_kv_latent, cache_k_rope, state_conv, page_table, g_ffn1, w1_gate, w1_up, w1_down, g_mix, w_in, g_q, w_q_b, g_kv, w_kv_b, w_dw, b_dw, g_cn, b_cn, g_out_attn, g_out_conv, w_out, g_ffn2, w2_gate, w2_up, w2_down, g_final):
    depth = g_ffn1.shape[0]
    assert depth == 1
    batch, seq, _ = x_prompt.shape
    db, t_new, _ = x_sample.shape
    n_pages = page_table.shape[1]
    past_len = n_pages * PAGE_SIZE
    l = 0
    row = lambda v: v.reshape(1, -1)

    win = jnp.concatenate(
        [w_in[l][:, :KR_OFF + ROPE_DIM], jnp.zeros((D_MODEL, LANES - ROPE_DIM), F32),
         w_in[l][:, KR_OFF + ROPE_DIM:]], axis=1).astype(BF16)
    wq_heads = jnp.concatenate([w_q_b[l][..., QK_NOPE:], w_q_b[l][..., :QK_NOPE]], axis=-1)
    wq = _pad_heads(wq_heads, 0).astype(BF16)
    w_uk = w_kv_b[l][..., :QK_NOPE]
    w_uv = w_kv_b[l][..., QK_NOPE:]
    wk_exp = _pad_heads(w_uk, ROPE_DIM).astype(BF16)
    wv_exp = _pad_heads(w_uv, 0).astype(BF16)
    wk_abs = jnp.pad(jnp.transpose(w_uk, (1, 2, 0)),
                     ((0, 0), (ROPE_DIM, HEAD_PAD - ROPE_DIM - QK_NOPE), (0, 0))).astype(BF16)
    eye = jnp.eye(N_HEADS, dtype=F32)
    wuv_bd = (jnp.pad(jnp.transpose(w_uv, (1, 0, 2)), ((0, 0), (0, 0), (0, HEAD_PAD - V_DIM)))
              [:, :, None, :] * eye[:, None, :, None]).reshape(N_HEADS * KV_RANK, D_HEADS_PAD)
    wuv_bd = wuv_bd.astype(BF16)
    g_attn_pad = _pad_heads(g_out_attn[l].reshape(N_HEADS, V_DIM), 0).reshape(1, D_HEADS_PAD)
    wo_attn = jnp.pad(w_out[l][:D_ATTN].reshape(N_HEADS, V_DIM, D_MODEL),
                      ((0, 0), (0, HEAD_PAD - V_DIM), (0, 0))).reshape(D_HEADS_PAD, D_MODEL)
    wo_attn = wo_attn.astype(BF16)
    wo_conv = w_out[l][D_ATTN:].astype(BF16)
    w1g, w1u, w1d = w1_gate[l].astype(BF16), w1_up[l].astype(BF16), w1_down[l].astype(BF16)
    w2g, w2u, w2d = w2_gate[l].astype(BF16), w2_up[l].astype(BF16), w2_down[l].astype(BF16)
    unused = jnp.zeros((SUBLANES, LANES), BF16)

    def stream(x, pos, conv_state, absorbed, attend):
        n = x.shape[0] * x.shape[1]
        nb, nt = x.shape[0], x.shape[1]
        x0 = x.reshape(n, D_MODEL)
        x1 = _ffn(x0, row(g_ffn1[l]), w1g, w1u, w1d, row(g_final), False)
        c, s1, s2 = _rope_tables(pos)
        wa, wb = (wk_abs, unused) if absorbed else (wk_exp, wv_exp)
        q, a, b, ckv, kpe, glu = _inproj(x1, row(g_mix[l]), win, row(g_q[l]), wq, row(g_kv[l]),
                                         wa, wb, c, s1, s2, absorbed)
        attn = attend(q, a, b, ckv, kpe)
        glu3 = glu.reshape(nb, nt, CONV_DIM)
        conv = _conv(glu3, conv_state, w_dw[l], row(b_dw[l]), row(g_cn[l]), row(b_cn[l]),
                     row(g_out_conv[l]), min(CONV_TILE, nt)).reshape(n, CONV_DIM)
        x2 = _outproj(x1, attn, conv, wuv_bd if absorbed else unused, g_attn_pad, wo_attn,
                      wo_conv, absorbed)
        y = _ffn(x2, row(g_ffn2[l]), w2g, w2u, w2d, row(g_final), True)
        new_conv = jnp.concatenate([conv_state, glu3], axis=1)[:, -CONV_STATE:]
        return y.reshape(nb, nt, D_MODEL), ckv, kpe, new_conv

    def attend_prompt(q, k, v, ckv, kpe):
        return _flash(q, k, v, batch, seq)

    pos_p = jnp.tile(jnp.arange(seq), batch)
    y_p, ckv_p, kpe_p, cv_p = stream(
        x_prompt, pos_p, jnp.zeros((batch, CONV_STATE, CONV_DIM), F32), False, attend_prompt)

    def attend_sample(q, q_abs, _, ckv, kpe):
        rows = t_new * N_HEADS
        o_lat = _paged_attention(
            page_table, q_abs.reshape(db, rows, KV_RANK), q.reshape(db, rows, HEAD_PAD),
            ckv.reshape(db, t_new, KV_RANK), kpe.reshape(db, t_new, ROPE_DIM),
            cache_kv_latent, cache_k_rope)
        return o_lat.reshape(db * t_new, N_HEADS * KV_RANK)

    pos_s = jnp.tile(past_len + jnp.arange(t_new), db)
    y_s, ckv_s, kpe_s, cv_s = stream(x_sample, pos_s, state_conv[l], True, attend_sample)

    n_pg = seq // PAGE_SIZE
    return (y_p, y_s,
            ckv_p.reshape(1, batch, n_pg, PAGE_SIZE, KV_RANK),
            kpe_p.reshape(1, batch, n_pg, PAGE_SIZE, ROPE_DIM),
            cv_p[None],
            ckv_s.reshape(1, db, t_new, KV_RANK),
            kpe_s.reshape(1, db, t_new, ROPE_DIM),
            cv_s[None])
```

```python
import functools

import jax
import jax.numpy as jnp
from jax import lax
from jax.experimental import pallas as pl
from jax.experimental.pallas import tpu as pltpu

D_MODEL = 1024
N_HEADS = 8
QK_NOPE = 64
ROPE_DIM = 32
HALF_ROPE = ROPE_DIM // 2
V_DIM = 64
Q_RANK = 384
KV_RANK = 256
CONV_DIM = 512
CONV_WIDTH = 31
CONV_STATE = CONV_WIDTH - 1
D_ATTN = N_HEADS * V_DIM
D_FF = 2816
PAGE_SIZE = 128
ROPE_THETA = 10000.0
EPS = 1e-6
SM_SCALE = (QK_NOPE + ROPE_DIM) ** -0.5
LOG2_E = 1.4426950408889634
NEG_INF = -1e30

LANES = 128
SUBLANES = 8
HEAD_PAD = LANES
D_HEADS_PAD = N_HEADS * HEAD_PAD
KR_OFF = Q_RANK + KV_RANK
GA_OFF = KR_OFF + LANES
GB_OFF = GA_OFF + CONV_DIM
IN_COLS_PAD = GB_OFF + CONV_DIM
VMEM_LIMIT = 56 * 1024 * 1024

TOKEN_TILE = 512
ATTN_TILE = TOKEN_TILE
HEADS_PER_STEP = 8
CONV_TILE = 512
CONV_BATCH = 16
CHUNK_KEYS = 2048
SCORES_AHEAD = 3

BF16 = jnp.bfloat16
F32 = jnp.float32


def _dot(a, b):
    return jnp.dot(a, b, preferred_element_type=F32)


def _dot_nt(a, b):
    return lax.dot_general(a, b, (((1,), (1,)), ((), ())), preferred_element_type=F32)


def _rms(x, g, axis=-1):
    ms = jnp.mean(x * x, axis=axis, keepdims=True)
    return x * lax.rsqrt(ms + EPS) * g


def _rope_tile(blk, c, s1, s2):
    return (blk * c + pltpu.roll(blk, HALF_ROPE, 1) * s1
            + pltpu.roll(blk, LANES - HALF_ROPE, 1) * s2)


def _const_spec(shape):
    nd = len(shape)
    return pl.BlockSpec(shape, lambda *_: (0,) * nd, pipeline_mode=pl.Buffered(1))


def _params(sem):
    return pltpu.CompilerParams(dimension_semantics=sem, vmem_limit_bytes=VMEM_LIMIT)


def _ffn_kernel(x_ref, g_ref, wg_ref, wu_ref, wd_ref, gf_ref, o_ref, *, final_norm):
    x = x_ref[...]
    h = _rms(x, g_ref[...]).astype(BF16)
    a = _dot(h, wg_ref[...])
    a = a * jax.nn.sigmoid(a) * _dot(h, wu_ref[...])
    y = x + 0.5 * _dot(a.astype(BF16), wd_ref[...])
    if final_norm:
        y = _rms(y, gf_ref[...])
    o_ref[...] = y


def _ffn(x, g, wg, wu, wd, gf, final_norm):
    n = x.shape[0]
    tm = TOKEN_TILE
    tok = pl.BlockSpec((tm, D_MODEL), lambda i: (i, 0))
    return pl.pallas_call(
        functools.partial(_ffn_kernel, final_norm=final_norm),
        out_shape=jax.ShapeDtypeStruct((n, D_MODEL), F32),
        grid=(n // tm,),
        in_specs=[tok, _const_spec((1, D_MODEL)), _const_spec((D_MODEL, D_FF)),
                  _const_spec((D_MODEL, D_FF)), _const_spec((D_FF, D_MODEL)),
                  _const_spec((1, D_MODEL))],
        out_specs=tok,
        compiler_params=_params(("parallel",)),
        name="ffn_final" if final_norm else "ffn",
    )(x, g, wg, wu, wd, gf)


def _inproj_kernel(x_ref, gmix_ref, win_ref, gq_ref, wq_ref, gkv_ref, wa_ref, wb_ref,
                   c_ref, s1_ref, s2_ref, q_ref, ckv_ref, kpe_ref, glu_ref, *extra, absorbed):
    x = x_ref[...]
    h = _rms(x, gmix_ref[...]).astype(BF16)
    u = _dot(h, win_ref[...])
    c, s1, s2 = c_ref[...], s1_ref[...], s2_ref[...]

    qn = _rms(u[:, :Q_RANK], gq_ref[...]).astype(BF16)
    q_raw = _dot(qn, wq_ref[...])
    ckv = _rms(u[:, Q_RANK:KR_OFF], gkv_ref[...])
    ckv_ref[...] = ckv
    ckv_b = ckv.astype(BF16)
    kpe = _rope_tile(u[:, KR_OFF:GA_OFF], c, s1, s2)
    kpe_ref[...] = kpe[:, :ROPE_DIM]
    glu_ref[...] = u[:, GA_OFF:GB_OFF] * jax.nn.sigmoid(u[:, GB_OFF:IN_COLS_PAD])

    if absorbed:
        (qabs_ref,) = extra
    else:
        k_ref, vt_ref = extra
        k_nope = _dot(ckv_b, wa_ref[...])
        vt_ref[0] = _dot_nt(wb_ref[...], ckv_b).astype(BF16)
    q_scale = SM_SCALE if absorbed else SM_SCALE * LOG2_E
    for hd in range(N_HEADS):
        sl = slice(hd * HEAD_PAD, (hd + 1) * HEAD_PAD)
        qh = (_rope_tile(q_raw[:, sl], c, s1, s2) * q_scale).astype(BF16)
        q_ref[:, sl] = qh
        if absorbed:
            qabs_ref[:, hd * KV_RANK:(hd + 1) * KV_RANK] = _dot(qh, wa_ref[hd]).astype(BF16)
        else:
            k_ref[:, sl] = (k_nope[:, sl] + kpe).astype(BF16)


def _inproj(x, gmix, win, gq, wq, gkv, wa, wb, tables, absorbed):
    n = x.shape[0]
    tm = TOKEN_TILE
    row = lambda w: pl.BlockSpec((tm, w), lambda i: (i, 0))
    table_blocks = tables[0].shape[0] // tm
    table = pl.BlockSpec((tm, LANES), lambda i: (i % table_blocks, 0))
    out_shape = [
        jax.ShapeDtypeStruct((n, D_HEADS_PAD), BF16),
        jax.ShapeDtypeStruct((n, KV_RANK), F32),
        jax.ShapeDtypeStruct((n, ROPE_DIM), F32),
        jax.ShapeDtypeStruct((n, CONV_DIM), F32),
    ]
    out_specs = [row(D_HEADS_PAD), row(KV_RANK), row(ROPE_DIM), row(CONV_DIM)]
    if absorbed:
        out_shape.append(jax.ShapeDtypeStruct((n, N_HEADS * KV_RANK), BF16))
        out_specs.append(row(N_HEADS * KV_RANK))
    else:
        out_shape += [jax.ShapeDtypeStruct((n, D_HEADS_PAD), BF16),
                      jax.ShapeDtypeStruct((n // tm, D_ATTN, tm), BF16)]
        out_specs += [row(D_HEADS_PAD), pl.BlockSpec((1, D_ATTN, tm), lambda i: (i, 0, 0))]
    return pl.pallas_call(
        functools.partial(_inproj_kernel, absorbed=absorbed),
        out_shape=out_shape,
        grid=(n // tm,),
        in_specs=[row(D_MODEL), _const_spec((1, D_MODEL)), _const_spec(win.shape),
                  _const_spec((1, Q_RANK)), _const_spec(wq.shape), _const_spec((1, KV_RANK)),
                  _const_spec(wa.shape), _const_spec(wb.shape), table, table, table],
        out_specs=out_specs,
        compiler_params=_params(("parallel",)),
        name="inproj_absorbed" if absorbed else "inproj",
    )(x, gmix, win, gq, wq, gkv, wa, wb, *tables)


def _flash_kernel(q_ref, k_ref, vt_ref, o_ref):
    t = ATTN_TILE
    qi = pl.program_id(2)
    heads = range(HEADS_PER_STEP)
    q = [q_ref[:, h * HEAD_PAD:(h + 1) * HEAD_PAD] for h in heads]

    def tile(j, carry, masked):
        start = pl.multiple_of(j * t, t)

        def scores(h):
            k = k_ref[pl.ds(start, t), h * HEAD_PAD:(h + 1) * HEAD_PAD]
            return _dot_nt(k, q[h])

        ahead = [scores(h) for h in range(min(SCORES_AHEAD, HEADS_PER_STEP))]
        out = []
        for h in heads:
            if h + SCORES_AHEAD < HEADS_PER_STEP:
                ahead.append(scores(h + SCORES_AHEAD))
            m, l, acc = carry[h]
            s = ahead[h]
            vt = vt_ref[j, h * V_DIM:(h + 1) * V_DIM, :]
            if masked:
                keys = lax.broadcasted_iota(jnp.int32, (t, t), 0)
                queries = lax.broadcasted_iota(jnp.int32, (t, t), 1)
                s = jnp.where(queries >= keys, s, NEG_INF)
            m_new = jnp.maximum(m, jnp.max(s, axis=0, keepdims=True))
            alpha = jnp.exp2(m - m_new)
            p = jnp.exp2(s - m_new)
            l = alpha * l + jnp.sum(p, axis=0, keepdims=True)
            acc = alpha * acc + _dot(vt, p.astype(BF16))
            out.append((m_new, l, acc))
        return tuple(out)

    init = tuple((jnp.full((1, t), NEG_INF, F32), jnp.zeros((1, t), F32),
                  jnp.zeros((V_DIM, t), F32)) for _ in heads)
    carry = lax.fori_loop(0, qi, lambda j, cr: tile(j, cr, False), init)
    carry = tile(qi, carry, True)
    for h in heads:
        m, l, acc = carry[h]
        o_ref[:, h * V_DIM:(h + 1) * V_DIM] = (acc / l).T


def _flash(q, k, vt, batch, seq):
    t = ATTN_TILE
    nq = seq // t
    width = HEADS_PER_STEP * HEAD_PAD
    rows = HEADS_PER_STEP * V_DIM
    return pl.pallas_call(
        _flash_kernel,
        out_shape=jax.ShapeDtypeStruct((batch * seq, D_ATTN), F32),
        grid=(batch, N_HEADS // HEADS_PER_STEP, nq),
        in_specs=[pl.BlockSpec((t, width), lambda b, h, i: (b * nq + i, h)),
                  pl.BlockSpec((seq, width), lambda b, h, i: (b, h)),
                  pl.BlockSpec((nq, rows, t), lambda b, h, i: (b, h, 0))],
        out_specs=pl.BlockSpec((t, rows), lambda b, h, i: (b * nq + i, h)),
        compiler_params=_params(("parallel", "parallel", "arbitrary")),
        name="flash_prompt",
    )(q, k, vt)


def _paged_kernel(pt_ref, qa_ref, qp_ref, cn_ref, kn_ref, kv_hbm, kr_hbm, o_ref,
                  kvbuf, krbuf, sem, *, n_pages, n_new):
    b = pl.program_id(0)
    nb = pl.num_programs(0)
    slot = b % 2
    rows = qa_ref.shape[1]

    def page_copies(bb, sl, j, off):
        page = pt_ref[bb * n_pages + j]
        return (pltpu.make_async_copy(kv_hbm.at[0, page],
                                      kvbuf.at[sl, pl.ds(off, PAGE_SIZE), :], sem.at[sl, 0]),
                pltpu.make_async_copy(kr_hbm.at[0, page],
                                      krbuf.at[sl, :, pl.ds(off, PAGE_SIZE)], sem.at[sl, 1]))

    def wait_all(sl):
        pltpu.make_async_copy(kvbuf.at[sl], kvbuf.at[sl], sem.at[sl, 0]).wait()
        pltpu.make_async_copy(krbuf.at[sl], krbuf.at[sl], sem.at[sl, 1]).wait()

    @pl.when(b == 0)
    def _():
        def body(j, carry):
            for cp in page_copies(0, 0, j, pl.multiple_of(j * PAGE_SIZE, PAGE_SIZE)):
                cp.start()
            return carry
        lax.fori_loop(0, n_pages, body, 0)

    nxt = (b + 1) % nb
    for j in range(n_pages):
        for cp in page_copies(nxt, 1 - slot, j, j * PAGE_SIZE):
            cp.start()
    wait_all(slot)

    qa = qa_ref[0].astype(F32)
    qp = qp_ref[0][:, :ROPE_DIM].astype(F32)

    cn = cn_ref[0]
    s = _dot_nt(qa, cn) + _dot_nt(qp, kn_ref[0])
    tq = lax.broadcasted_iota(jnp.int32, (rows, n_new), 0) // N_HEADS
    tk = lax.broadcasted_iota(jnp.int32, (rows, n_new), 1)
    s = jnp.where(tq >= tk, s, NEG_INF)
    m = jnp.max(s, axis=-1, keepdims=True)
    p = jnp.exp(s - m)
    parts = [(m, jnp.sum(p, axis=-1, keepdims=True), _dot(p, cn))]

    n_chunks = n_pages * PAGE_SIZE // CHUNK_KEYS

    def chunk_kv(c):
        return kvbuf[slot, c * CHUNK_KEYS:(c + 1) * CHUNK_KEYS, :]

    def scores(c):
        krt = krbuf[slot, :, c * CHUNK_KEYS:(c + 1) * CHUNK_KEYS]
        return _dot_nt(qa, chunk_kv(c)) + _dot(qp, krt)

    ahead = [scores(c) for c in range(min(SCORES_AHEAD, n_chunks))]
    for c in range(n_chunks):
        if c + SCORES_AHEAD < n_chunks:
            ahead.append(scores(c + SCORES_AHEAD))
        s = ahead[c]
        m = jnp.max(s, axis=-1, keepdims=True)
        p = jnp.exp(s - m)
        parts.append((m, jnp.sum(p, axis=-1, keepdims=True), _dot(p, chunk_kv(c))))

    m_all = functools.reduce(jnp.maximum, [m for m, _, _ in parts])
    l = jnp.zeros((rows, 1), F32)
    acc = jnp.zeros((rows, KV_RANK), F32)
    for m, l_c, acc_c in parts:
        w = jnp.exp(m - m_all)
        l = l + w * l_c
        acc = acc + w * acc_c
    o_ref[0] = acc / l

    @pl.when(b == nb - 1)
    def _():
        wait_all(1 - slot)


def _paged_attention(page_table, q_abs, q_rope, c_new, k_new, cache_kv, cache_kr_t):
    db, rows = q_abs.shape[0], q_abs.shape[1]
    n_new = c_new.shape[1]
    n_pages = page_table.shape[1]
    chunk_keys = n_pages * PAGE_SIZE
    assert chunk_keys % CHUNK_KEYS == 0
    per_b = lambda r, w: pl.BlockSpec((1, r, w), lambda b, pt: (b, 0, 0))
    grid_spec = pltpu.PrefetchScalarGridSpec(
        num_scalar_prefetch=1,
        grid=(db,),
        in_specs=[per_b(rows, KV_RANK), per_b(rows, HEAD_PAD), per_b(n_new, KV_RANK),
                  per_b(n_new, ROPE_DIM), pl.BlockSpec(memory_space=pl.ANY),
                  pl.BlockSpec(memory_space=pl.ANY)],
        out_specs=per_b(rows, KV_RANK),
        scratch_shapes=[pltpu.VMEM((2, chunk_keys, KV_RANK), F32),
                        pltpu.VMEM((2, ROPE_DIM, chunk_keys), F32),
                        pltpu.SemaphoreType.DMA((2, 2))],
    )
    return pl.pallas_call(
        functools.partial(_paged_kernel, n_pages=n_pages, n_new=n_new),
        out_shape=jax.ShapeDtypeStruct((db, rows, KV_RANK), F32),
        grid_spec=grid_spec,
        compiler_params=_params(("arbitrary",)),
        name="paged_attention",
    )(page_table.reshape(-1), q_abs, q_rope, c_new, k_new, cache_kv, cache_kr_t)


EXT_HEAD = 32


def _conv_kernel(g_ref, st_ref, w_ref, bdw_ref, gcn_ref, bcn_ref, gout_ref, o_ref, ext_ref, *, tt):
    ti = pl.program_id(1)
    pad = EXT_HEAD - CONV_STATE

    @pl.when(ti == 0)
    def _():
        ext_ref[:, pad:EXT_HEAD, :] = st_ref[...]

    @pl.when(ti > 0)
    def _():
        ext_ref[:, 0:EXT_HEAD, :] = ext_ref[:, tt:tt + EXT_HEAD, :]

    ext_ref[:, EXT_HEAD:EXT_HEAD + tt, :] = g_ref[...]
    acc = jnp.zeros(g_ref.shape, F32) + bdw_ref[...]
    for w in range(CONV_WIDTH):
        acc = acc + ext_ref[:, pad + w:pad + w + tt, :] * w_ref[w:w + 1, :]
    mu = jnp.mean(acc, axis=-1, keepdims=True)
    d = acc - mu
    var = jnp.mean(d * d, axis=-1, keepdims=True)
    y = d * lax.rsqrt(var + EPS) * gcn_ref[...] + bcn_ref[...]
    y = y * jax.nn.sigmoid(y)
    o_ref[...] = _rms(y, gout_ref[...])


def _conv(glu, state, w_dw, b_dw, g_cn, b_cn, g_out, tt, gb):
    b, t = glu.shape[0], glu.shape[1]
    vec = _const_spec((1, CONV_DIM))
    return pl.pallas_call(
        functools.partial(_conv_kernel, tt=tt),
        out_shape=jax.ShapeDtypeStruct((b, t, CONV_DIM), F32),
        grid=(b // gb, t // tt),
        in_specs=[pl.BlockSpec((gb, tt, CONV_DIM), lambda i, j: (i, j, 0)),
                  pl.BlockSpec((None, gb, CONV_STATE, CONV_DIM), lambda i, j: (0, i, 0, 0)),
                  _const_spec(w_dw.shape), vec, vec, vec, vec],
        out_specs=pl.BlockSpec((gb, tt, CONV_DIM), lambda i, j: (i, j, 0)),
        scratch_shapes=[pltpu.VMEM((gb, EXT_HEAD + tt, CONV_DIM), F32)],
        compiler_params=_params(("parallel", "arbitrary")),
        name="conv_module",
    )(glu, state, w_dw, b_dw, g_cn, b_cn, g_out)


def _outproj_kernel(x_ref, a_ref, cv_ref, wuv_ref, ga_ref, woa_ref, woc_ref, o_ref, *, absorbed):
    if absorbed:
        attn = _dot(a_ref[...].astype(BF16), wuv_ref[...])
    else:
        attn = a_ref[...]
    an = _rms(attn, ga_ref[...]).astype(BF16)
    o_ref[...] = (x_ref[...] + _dot(an, woa_ref[...])
                  + _dot(cv_ref[...].astype(BF16), woc_ref[...]))


def _outproj(x, attn, conv, wuv, ga, woa, woc, absorbed):
    n = x.shape[0]
    tm = TOKEN_TILE
    row = lambda w: pl.BlockSpec((tm, w), lambda i: (i, 0))
    return pl.pallas_call(
        functools.partial(_outproj_kernel, absorbed=absorbed),
        out_shape=jax.ShapeDtypeStruct((n, D_MODEL), F32),
        grid=(n // tm,),
        in_specs=[row(D_MODEL), row(attn.shape[1]), row(CONV_DIM), _const_spec(wuv.shape),
                  _const_spec((1, D_ATTN)), _const_spec(woa.shape), _const_spec(woc.shape)],
        out_specs=row(D_MODEL),
        compiler_params=_params(("parallel",)),
        name="outproj_absorbed" if absorbed else "outproj",
    )(x, attn, conv, wuv, ga, woa, woc)


def _pad_heads(w, lo):
    d = w.shape[-1]
    w = jnp.pad(w, [(0, 0)] * (w.ndim - 1) + [(lo, HEAD_PAD - lo - d)])
    return w.reshape(*w.shape[:-2], D_HEADS_PAD)


def _rope_tables(pos):
    inv_freq = ROPE_THETA ** (-jnp.arange(0, ROPE_DIM, 2, dtype=F32) / ROPE_DIM)
    inv_lanes = jnp.concatenate([inv_freq, inv_freq, jnp.zeros((LANES - ROPE_DIM,), F32)])
    ang = pos.astype(F32)[:, None] * inv_lanes[None, :]
    lane = jnp.arange(LANES)[None, :]
    sin = jnp.sin(ang)
    zero = jnp.zeros_like(sin)
    s1 = jnp.where((lane >= HALF_ROPE) & (lane < ROPE_DIM), sin, zero)
    s2 = jnp.where(lane < HALF_ROPE, -sin, zero)
    return jnp.cos(ang), s1, s2


def kernel(x_prompt, x_sample, cache_kv_latent, cache_k_rope, state_conv, page_table, g_ffn1, w1_gate, w1_up, w1_down, g_mix, w_in, g_q, w_q_b, g_kv, w_kv_b, w_dw, b_dw, g_cn, b_cn, g_out_attn, g_out_conv, w_out, g_ffn2, w2_gate, w2_up, w2_down, g_final):
    depth = g_ffn1.shape[0]
    assert depth == 1
    batch, seq, _ = x_prompt.shape
    db, t_new, _ = x_sample.shape
    n_pages = page_table.shape[1]
    past_len = n_pages * PAGE_SIZE
    assert seq % TOKEN_TILE == 0 and TOKEN_TILE % t_new == 0 and (db * t_new) % TOKEN_TILE == 0
    l = 0
    row = lambda v: v.reshape(1, -1)

    win = jnp.concatenate(
        [w_in[l][:, :KR_OFF + ROPE_DIM], jnp.zeros((D_MODEL, LANES - ROPE_DIM), F32),
         w_in[l][:, KR_OFF + ROPE_DIM:]], axis=1).astype(BF16)
    wq_heads = jnp.concatenate([w_q_b[l][..., QK_NOPE:], w_q_b[l][..., :QK_NOPE]], axis=-1)
    wq = _pad_heads(wq_heads, 0).astype(BF16)
    w_uk = w_kv_b[l][..., :QK_NOPE]
    w_uv = w_kv_b[l][..., QK_NOPE:]
    wk_exp = _pad_heads(w_uk, ROPE_DIM).astype(BF16)
    wv_t = w_uv.reshape(KV_RANK, D_ATTN).T.astype(BF16)
    wk_abs = jnp.pad(jnp.transpose(w_uk, (1, 2, 0)),
                     ((0, 0), (ROPE_DIM, HEAD_PAD - ROPE_DIM - QK_NOPE), (0, 0))).astype(BF16)
    eye = jnp.eye(N_HEADS, dtype=F32)
    wuv_bd = (jnp.transpose(w_uv, (1, 0, 2))[:, :, None, :] * eye[:, None, :, None])
    wuv_bd = wuv_bd.reshape(N_HEADS * KV_RANK, D_ATTN).astype(BF16)
    wo_attn = w_out[l][:D_ATTN].astype(BF16)
    wo_conv = w_out[l][D_ATTN:].astype(BF16)
    w1g, w1u, w1d = w1_gate[l].astype(BF16), w1_up[l].astype(BF16), w1_down[l].astype(BF16)
    w2g, w2u, w2d = w2_gate[l].astype(BF16), w2_up[l].astype(BF16), w2_down[l].astype(BF16)
    unused = jnp.zeros((SUBLANES, LANES), BF16)

    def stream(x, tables, conv_state, conv_batch, absorbed, attend):
        nb, nt = x.shape[0], x.shape[1]
        n = nb * nt
        x1 = _ffn(x.reshape(n, D_MODEL), row(g_ffn1[l]), w1g, w1u, w1d, row(g_final), False)
        wa, wb = (wk_abs, unused) if absorbed else (wk_exp, wv_t)
        q, ckv, kpe, glu, *extra = _inproj(x1, row(g_mix[l]), win, row(g_q[l]), wq, row(g_kv[l]),
                                           wa, wb, tables, absorbed)
        attn = attend(q, ckv, kpe, *extra)
        glu3 = glu.reshape(nb, nt, CONV_DIM)
        conv = _conv(glu3, conv_state, w_dw[l], row(b_dw[l]), row(g_cn[l]), row(b_cn[l]),
                     row(g_out_conv[l]), min(CONV_TILE, nt), conv_batch).reshape(n, CONV_DIM)
        x2 = _outproj(x1, attn, conv, wuv_bd if absorbed else unused, row(g_out_attn[l]),
                      wo_attn, wo_conv, absorbed)
        y = _ffn(x2, row(g_ffn2[l]), w2g, w2u, w2d, row(g_final), True)
        new_conv = jnp.concatenate([conv_state[0], glu3], axis=1)[:, -CONV_STATE:]
        return y.reshape(nb, nt, D_MODEL), ckv, kpe, new_conv

    def attend_prompt(q, ckv, kpe, k, vt):
        return _flash(q, k, vt, batch, seq)

    y_p, ckv_p, kpe_p, cv_p = stream(
        x_prompt, _rope_tables(jnp.arange(seq)),
        jnp.zeros((1, batch, CONV_STATE, CONV_DIM), F32), 1, False, attend_prompt)

    cache_kr_t = jnp.swapaxes(cache_k_rope, 2, 3)

    def attend_sample(q, ckv, kpe, q_abs):
        rows = t_new * N_HEADS
        o_lat = _paged_attention(
            page_table, q_abs.reshape(db, rows, KV_RANK), q.reshape(db, rows, HEAD_PAD),
            ckv.reshape(db, t_new, KV_RANK), kpe.reshape(db, t_new, ROPE_DIM),
            cache_kv_latent, cache_kr_t)
        return o_lat.reshape(db * t_new, N_HEADS * KV_RANK)

    pos_s = jnp.tile(past_len + jnp.arange(t_new), TOKEN_TILE // t_new)
    y_s, ckv_s, kpe_s, cv_s = stream(x_sample, _rope_tables(pos_s), state_conv,
                                     CONV_BATCH, True, attend_sample)

    n_pg = seq // PAGE_SIZE
    return (y_p, y_s,
            ckv_p.reshape(1, batch, n_pg, PAGE_SIZE, KV_RANK),
            kpe_p.reshape(1, batch, n_pg, PAGE_SIZE, ROPE_DIM),
            cv_p[None],
            ckv_s.reshape(1, db, t_new, KV_RANK),
            kpe_s.reshape(1, db, t_new, ROPE_DIM),
            cv_s[None])
```

```python
import functools

import jax
import jax.numpy as jnp
from jax import lax
from jax.experimental import pallas as pl
from jax.experimental.pallas import tpu as pltpu

D_MODEL = 1024
N_HEADS = 8
QK_NOPE = 64
ROPE_DIM = 32
HALF_ROPE = ROPE_DIM // 2
V_DIM = 64
Q_RANK = 384
KV_RANK = 256
CONV_DIM = 512
CONV_WIDTH = 31
CONV_STATE = CONV_WIDTH - 1
D_ATTN = N_HEADS * V_DIM
D_FF = 2816
PAGE_SIZE = 128
ROPE_THETA = 10000.0
EPS = 1e-6
SM_SCALE = (QK_NOPE + ROPE_DIM) ** -0.5
LOG2_E = 1.4426950408889634
NEG_INF = -1e30

LANES = 128
SUBLANES = 8
HEAD_PAD = LANES
D_HEADS_PAD = N_HEADS * HEAD_PAD
KR_OFF = Q_RANK + KV_RANK
GA_OFF = KR_OFF + LANES
GB_OFF = GA_OFF + CONV_DIM
IN_COLS_PAD = GB_OFF + CONV_DIM
VMEM_LIMIT = 56 * 1024 * 1024

TOKEN_TILE = 512
ATTN_TILE = TOKEN_TILE
ONES_ROWS = 16
HEADS_AHEAD = 2
HEADS_PER_STEP = 8
CONV_TILE = 512
CONV_ROWS = 64
CONV_BATCH = 16
CHUNK_KEYS = 2048
SCORES_AHEAD = 3

BF16 = jnp.bfloat16
F32 = jnp.float32


def _dot(a, b):
    return jnp.dot(a, b, preferred_element_type=F32)


def _dot_nt(a, b):
    return lax.dot_general(a, b, (((1,), (1,)), ((), ())), preferred_element_type=F32)


def _rms(x, g, axis=-1):
    ms = jnp.mean(x * x, axis=axis, keepdims=True)
    return x * lax.rsqrt(ms + EPS) * g


def _rope_tile(blk, c, s1, s2):
    return (blk * c + pltpu.roll(blk, HALF_ROPE, 1) * s1
            + pltpu.roll(blk, LANES - HALF_ROPE, 1) * s2)


def _const_spec(shape):
    nd = len(shape)
    return pl.BlockSpec(shape, lambda *_: (0,) * nd, pipeline_mode=pl.Buffered(1))


def _params(sem):
    return pltpu.CompilerParams(dimension_semantics=sem, vmem_limit_bytes=VMEM_LIMIT)


def _half_ffn(x, g_ref, wg_ref, wu_ref, wd_ref):
    h = _rms(x, g_ref[...]).astype(BF16)
    a = _dot(h, wg_ref[...])
    a = a * jax.nn.sigmoid(a) * _dot(h, wu_ref[...])
    return x + 0.5 * _dot(a.astype(BF16), wd_ref[...])


def _ffn_kernel(x_ref, g_ref, wg_ref, wu_ref, wd_ref, o_ref):
    o_ref[...] = _half_ffn(x_ref[...], g_ref, wg_ref, wu_ref, wd_ref)


def _ffn_weight_specs():
    return [_const_spec((1, D_MODEL)), _const_spec((D_MODEL, D_FF)),
            _const_spec((D_MODEL, D_FF)), _const_spec((D_FF, D_MODEL))]


def _ffn(x, g, wg, wu, wd):
    n = x.shape[0]
    tm = TOKEN_TILE
    tok = pl.BlockSpec((tm, D_MODEL), lambda i: (i, 0))
    return pl.pallas_call(
        _ffn_kernel,
        out_shape=jax.ShapeDtypeStruct((n, D_MODEL), F32),
        grid=(n // tm,),
        in_specs=[tok] + _ffn_weight_specs(),
        out_specs=tok,
        compiler_params=_params(("parallel",)),
        name="ffn",
    )(x, g, wg, wu, wd)


def _merge_ffn_kernel(x_ref, a_ref, cv_ref, wuv_ref, ga_ref, woa_ref, woc_ref,
                      g_ref, wg_ref, wu_ref, wd_ref, gf_ref, o_ref, *, absorbed):
    if absorbed:
        attn = _dot(a_ref[...].astype(BF16), wuv_ref[...])
    else:
        attn = a_ref[...]
    an = _rms(attn, ga_ref[...]).astype(BF16)
    x = x_ref[...] + _dot(an, woa_ref[...]) + _dot(cv_ref[...].astype(BF16), woc_ref[...])
    o_ref[...] = _rms(_half_ffn(x, g_ref, wg_ref, wu_ref, wd_ref), gf_ref[...])


def _merge_ffn(x, attn, conv, wuv, ga, woa, woc, g, wg, wu, wd, gf, absorbed):
    n = x.shape[0]
    tm = TOKEN_TILE
    row = lambda w: pl.BlockSpec((tm, w), lambda i: (i, 0))
    return pl.pallas_call(
        functools.partial(_merge_ffn_kernel, absorbed=absorbed),
        out_shape=jax.ShapeDtypeStruct((n, D_MODEL), F32),
        grid=(n // tm,),
        in_specs=[row(D_MODEL), row(attn.shape[1]), row(CONV_DIM), _const_spec(wuv.shape),
                  _const_spec((1, D_ATTN)), _const_spec(woa.shape), _const_spec(woc.shape)]
                 + _ffn_weight_specs() + [_const_spec((1, D_MODEL))],
        out_specs=row(D_MODEL),
        compiler_params=_params(("parallel",)),
        name="merge_ffn_absorbed" if absorbed else "merge_ffn",
    )(x, attn, conv, wuv, ga, woa, woc, g, wg, wu, wd, gf)


def _inproj_kernel(x_ref, gmix_ref, win_ref, gq_ref, wq_ref, gkv_ref, wa_ref, wb_ref,
                   c_ref, s1_ref, s2_ref, q_ref, ckv_ref, kpe_ref, glu_ref, *extra, absorbed):
    x = x_ref[...]
    h = _rms(x, gmix_ref[...]).astype(BF16)
    u = _dot(h, win_ref[...])
    c, s1, s2 = c_ref[...], s1_ref[...], s2_ref[...]

    qn = _rms(u[:, :Q_RANK], gq_ref[...]).astype(BF16)
    q_raw = _dot(qn, wq_ref[...])
    ckv = _rms(u[:, Q_RANK:KR_OFF], gkv_ref[...])
    ckv_ref[...] = ckv
    ckv_b = ckv.astype(BF16)
    kpe = _rope_tile(u[:, KR_OFF:GA_OFF], c, s1, s2)
    kpe_ref[...] = kpe[:, :ROPE_DIM]
    glu_ref[...] = u[:, GA_OFF:GB_OFF] * jax.nn.sigmoid(u[:, GB_OFF:IN_COLS_PAD])

    if absorbed:
        (qabs_ref,) = extra
    else:
        k_ref, vt_ref = extra
        k_nope = _dot(ckv_b, wa_ref[...])
        vt_ref[0] = _dot_nt(wb_ref[...], ckv_b).astype(BF16)
    q_scale = SM_SCALE if absorbed else SM_SCALE * LOG2_E
    for hd in range(N_HEADS):
        sl = slice(hd * HEAD_PAD, (hd + 1) * HEAD_PAD)
        qh = (_rope_tile(q_raw[:, sl], c, s1, s2) * q_scale).astype(BF16)
        q_ref[:, sl] = qh.astype(q_ref.dtype)
        if absorbed:
            qabs_ref[:, hd * KV_RANK:(hd + 1) * KV_RANK] = (
                _dot(qh, wa_ref[hd]).astype(BF16).astype(F32))
        else:
            k_ref[:, sl] = (k_nope[:, sl] + kpe).astype(BF16)


def _inproj(x, gmix, win, gq, wq, gkv, wa, wb, tables, absorbed):
    n = x.shape[0]
    tm = TOKEN_TILE
    row = lambda w: pl.BlockSpec((tm, w), lambda i: (i, 0))
    table_blocks = tables[0].shape[0] // tm
    table = pl.BlockSpec((tm, LANES), lambda i: (i % table_blocks, 0))
    out_shape = [
        jax.ShapeDtypeStruct((n, D_HEADS_PAD), F32 if absorbed else BF16),
        jax.ShapeDtypeStruct((n, KV_RANK), F32),
        jax.ShapeDtypeStruct((n, ROPE_DIM), F32),
        jax.ShapeDtypeStruct((n, CONV_DIM), F32),
    ]
    out_specs = [row(D_HEADS_PAD), row(KV_RANK), row(ROPE_DIM), row(CONV_DIM)]
    if absorbed:
        out_shape.append(jax.ShapeDtypeStruct((n, N_HEADS * KV_RANK), F32))
        out_specs.append(row(N_HEADS * KV_RANK))
    else:
        out_shape += [jax.ShapeDtypeStruct((n, D_HEADS_PAD), BF16),
                      jax.ShapeDtypeStruct((n // tm, D_ATTN, tm), BF16)]
        out_specs += [row(D_HEADS_PAD), pl.BlockSpec((1, D_ATTN, tm), lambda i: (i, 0, 0))]
    return pl.pallas_call(
        functools.partial(_inproj_kernel, absorbed=absorbed),
        out_shape=out_shape,
        grid=(n // tm,),
        in_specs=[row(D_MODEL), _const_spec((1, D_MODEL)), _const_spec(win.shape),
                  _const_spec((1, Q_RANK)), _const_spec(wq.shape), _const_spec((1, KV_RANK)),
                  _const_spec(wa.shape), _const_spec(wb.shape), table, table, table],
        out_specs=out_specs,
        compiler_params=_params(("parallel",)),
        name="inproj_absorbed" if absorbed else "inproj",
    )(x, gmix, win, gq, wq, gkv, wa, wb, *tables)


def _flash_kernel(q_ref, k_ref, vt_ref, o_ref, m_ref, acc_ref):
    t = ATTN_TILE
    qi = pl.program_id(2)
    heads = range(HEADS_PER_STEP)
    q = [q_ref[:, h * HEAD_PAD:(h + 1) * HEAD_PAD] for h in heads]
    m_ref[...] = jnp.full(m_ref.shape, NEG_INF, F32)
    acc_ref[...] = jnp.zeros(acc_ref.shape, F32)

    def tile(j, masked):
        start = pl.multiple_of(j * t, t)

        def scores(h):
            k = k_ref[pl.ds(start, t), h * HEAD_PAD:(h + 1) * HEAD_PAD]
            return _dot_nt(k, q[h])

        ahead = [scores(h) for h in range(min(HEADS_AHEAD, HEADS_PER_STEP))]
        ones = jnp.ones((ONES_ROWS, t), BF16)
        for h in heads:
            if h + HEADS_AHEAD < HEADS_PER_STEP:
                ahead.append(scores(h + HEADS_AHEAD))
            s = ahead[h]
            vt = jnp.concatenate([vt_ref[j, h * V_DIM:(h + 1) * V_DIM, :], ones], axis=0)
            if masked:
                keys = lax.broadcasted_iota(jnp.int32, (t, t), 0)
                queries = lax.broadcasted_iota(jnp.int32, (t, t), 1)
                s = jnp.where(queries >= keys, s, NEG_INF)
            m = m_ref[h]
            m_new = jnp.maximum(m, jnp.max(s, axis=0, keepdims=True))
            p = jnp.exp2(s - m_new).astype(BF16)
            acc_ref[h] = jnp.exp2(m - m_new) * acc_ref[h] + _dot(vt, p)
            m_ref[h] = m_new

    def body(j, carry):
        tile(j, False)
        return carry

    lax.fori_loop(0, qi, body, 0)
    tile(qi, True)
    for h in heads:
        acc = acc_ref[h]
        o_ref[:, h * V_DIM:(h + 1) * V_DIM] = (acc[:V_DIM] / acc[V_DIM:V_DIM + 1]).T


def _flash(q, k, vt, batch, seq):
    t = ATTN_TILE
    nq = seq // t
    width = HEADS_PER_STEP * HEAD_PAD
    rows = HEADS_PER_STEP * V_DIM
    return pl.pallas_call(
        _flash_kernel,
        out_shape=jax.ShapeDtypeStruct((batch * seq, D_ATTN), F32),
        grid=(batch, N_HEADS // HEADS_PER_STEP, nq),
        in_specs=[pl.BlockSpec((t, width), lambda b, h, i: (b * nq + i, h)),
                  pl.BlockSpec((seq, width), lambda b, h, i: (b, h)),
                  pl.BlockSpec((nq, rows, t), lambda b, h, i: (b, h, 0))],
        out_specs=pl.BlockSpec((t, rows), lambda b, h, i: (b * nq + i, h)),
        scratch_shapes=[pltpu.VMEM((HEADS_PER_STEP, 1, t), F32),
                        pltpu.VMEM((HEADS_PER_STEP, V_DIM + ONES_ROWS, t), F32)],
        compiler_params=_params(("parallel", "parallel", "arbitrary")),
        name="flash_prompt",
    )(q, k, vt)


def _paged_kernel(pt_ref, qa_ref, qp_ref, cn_ref, kn_ref, kv_hbm, kr_hbm, o_ref,
                  kvbuf, krbuf, sem, *, n_pages, n_new):
    b = pl.program_id(0)
    nb = pl.num_programs(0)
    slot = b % 2
    rows = n_new * N_HEADS

    def page_copies(bb, sl, j, off):
        page = pt_ref[bb * n_pages + j]
        return (pltpu.make_async_copy(kv_hbm.at[0, page],
                                      kvbuf.at[sl, pl.ds(off, PAGE_SIZE), :], sem.at[sl, 0]),
                pltpu.make_async_copy(kr_hbm.at[0, page],
                                      krbuf.at[sl, :, pl.ds(off, PAGE_SIZE)], sem.at[sl, 1]))

    def wait_all(sl):
        pltpu.make_async_copy(kvbuf.at[sl], kvbuf.at[sl], sem.at[sl, 0]).wait()
        pltpu.make_async_copy(krbuf.at[sl], krbuf.at[sl], sem.at[sl, 1]).wait()

    @pl.when(b == 0)
    def _():
        def body(j, carry):
            for cp in page_copies(0, 0, j, pl.multiple_of(j * PAGE_SIZE, PAGE_SIZE)):
                cp.start()
            return carry
        lax.fori_loop(0, n_pages, body, 0)

    nxt = (b + 1) % nb
    for j in range(n_pages):
        for cp in page_copies(nxt, 1 - slot, j, j * PAGE_SIZE):
            cp.start()
    wait_all(slot)

    qa_all = qa_ref[0]
    qp_all = qp_ref[0]
    qa = jnp.concatenate([qa_all[:, h * KV_RANK:(h + 1) * KV_RANK] for h in range(N_HEADS)], axis=0)
    qp = jnp.concatenate([qp_all[:, h * HEAD_PAD:h * HEAD_PAD + ROPE_DIM]
                          for h in range(N_HEADS)], axis=0)

    cn = cn_ref[0]
    s = _dot_nt(qa, cn) + _dot_nt(qp, kn_ref[0])
    tq = lax.broadcasted_iota(jnp.int32, (rows, n_new), 0) % n_new
    tk = lax.broadcasted_iota(jnp.int32, (rows, n_new), 1)
    s = jnp.where(tq >= tk, s, NEG_INF)
    m = jnp.max(s, axis=-1, keepdims=True)
    p = jnp.exp(s - m)
    parts = [(m, jnp.sum(p, axis=-1, keepdims=True), _dot(p, cn))]

    n_chunks = n_pages * PAGE_SIZE // CHUNK_KEYS

    def chunk_kv(c):
        return kvbuf[slot, c * CHUNK_KEYS:(c + 1) * CHUNK_KEYS, :]

    def scores(c):
        krt = krbuf[slot, :, c * CHUNK_KEYS:(c + 1) * CHUNK_KEYS]
        return _dot_nt(qa, chunk_kv(c)) + _dot(qp, krt)

    ahead = [scores(c) for c in range(min(SCORES_AHEAD, n_chunks))]
    for c in range(n_chunks):
        if c + SCORES_AHEAD < n_chunks:
            ahead.append(scores(c + SCORES_AHEAD))
        s = ahead[c]
        m = jnp.max(s, axis=-1, keepdims=True)
        p = jnp.exp(s - m)
        parts.append((m, jnp.sum(p, axis=-1, keepdims=True), _dot(p, chunk_kv(c))))

    m_all = functools.reduce(jnp.maximum, [m for m, _, _ in parts])
    l = jnp.zeros((rows, 1), F32)
    acc = jnp.zeros((rows, KV_RANK), F32)
    for m, l_c, acc_c in parts:
        w = jnp.exp(m - m_all)
        l = l + w * l_c
        acc = acc + w * acc_c
    o = acc / l
    for h in range(N_HEADS):
        o_ref[0, :, h * KV_RANK:(h + 1) * KV_RANK] = o[h * n_new:(h + 1) * n_new]

    @pl.when(b == nb - 1)
    def _():
        wait_all(1 - slot)


def _paged_attention(page_table, q_abs, q_rope, c_new, k_new, cache_kv, cache_kr_t):
    db, n_new = c_new.shape[0], c_new.shape[1]
    n_pages = page_table.shape[1]
    chunk_keys = n_pages * PAGE_SIZE
    assert chunk_keys % CHUNK_KEYS == 0 and n_new % SUBLANES == 0
    per_b = lambda w: pl.BlockSpec((1, n_new, w), lambda b, pt: (b, 0, 0))
    grid_spec = pltpu.PrefetchScalarGridSpec(
        num_scalar_prefetch=1,
        grid=(db,),
        in_specs=[per_b(N_HEADS * KV_RANK), per_b(D_HEADS_PAD), per_b(KV_RANK),
                  per_b(ROPE_DIM), pl.BlockSpec(memory_space=pl.ANY),
                  pl.BlockSpec(memory_space=pl.ANY)],
        out_specs=per_b(N_HEADS * KV_RANK),
        scratch_shapes=[pltpu.VMEM((2, chunk_keys, KV_RANK), F32),
                        pltpu.VMEM((2, ROPE_DIM, chunk_keys), F32),
                        pltpu.SemaphoreType.DMA((2, 2))],
    )
    return pl.pallas_call(
        functools.partial(_paged_kernel, n_pages=n_pages, n_new=n_new),
        out_shape=jax.ShapeDtypeStruct((db, n_new, N_HEADS * KV_RANK), F32),
        grid_spec=grid_spec,
        compiler_params=_params(("arbitrary",)),
        name="paged_attention",
    )(page_table.reshape(-1), q_abs, q_rope, c_new, k_new, cache_kv, cache_kr_t)


EXT_HEAD = 32


def _conv_kernel(g_ref, st_ref, w_ref, bdw_ref, gcn_ref, bcn_ref, gout_ref, o_ref, ext_ref, sh_ref,
                 *, tt):
    ti = pl.program_id(1)
    pad = EXT_HEAD - CONV_STATE

    @pl.when(ti == 0)
    def _():
        ext_ref[:, 0:pad, :] = jnp.zeros((g_ref.shape[0], pad, CONV_DIM), F32)
        ext_ref[:, pad:EXT_HEAD, :] = st_ref[...]

    @pl.when(ti > 0)
    def _():
        ext_ref[:, 0:EXT_HEAD, :] = ext_ref[:, tt:tt + EXT_HEAD, :]

    ext_ref[:, EXT_HEAD:EXT_HEAD + tt, :] = g_ref[...]
    rb = min(CONV_ROWS, tt)
    gb = g_ref.shape[0]

    span = tt + EXT_HEAD - SUBLANES
    for r in range(1, SUBLANES):
        sh_ref[r - 1, :, 0:span, :] = ext_ref[:, r:r + span, :]

    def block(i, carry):
        r0 = pl.multiple_of(i * rb, rb)
        acc = jnp.zeros((gb, rb, CONV_DIM), F32) + bdw_ref[...]
        for o in range(pad, EXT_HEAD + 1):
            a, r = divmod(o, SUBLANES)
            rows = pl.ds(r0 + SUBLANES * a, rb)
            src = ext_ref[:, rows, :] if r == 0 else sh_ref[r - 1, :, rows, :]
            acc = acc + src * w_ref[o - pad:o - pad + 1, :]
        o_ref[:, pl.ds(r0, rb), :] = acc
        return carry

    lax.fori_loop(0, tt // rb, block, 0)
    conv = o_ref[...]
    mu = jnp.mean(conv, axis=-1, keepdims=True)
    d = conv - mu
    var = jnp.mean(d * d, axis=-1, keepdims=True)
    y = d * lax.rsqrt(var + EPS) * gcn_ref[...] + bcn_ref[...]
    y = y * jax.nn.sigmoid(y)
    o_ref[...] = _rms(y, gout_ref[...])


def _conv(glu, state, w_dw, b_dw, g_cn, b_cn, g_out, tt, gb):
    b, t = glu.shape[0], glu.shape[1]
    vec = _const_spec((1, CONV_DIM))
    return pl.pallas_call(
        functools.partial(_conv_kernel, tt=tt),
        out_shape=jax.ShapeDtypeStruct((b, t, CONV_DIM), F32),
        grid=(b // gb, t // tt),
        in_specs=[pl.BlockSpec((gb, tt, CONV_DIM), lambda i, j: (i, j, 0)),
                  pl.BlockSpec((None, gb, CONV_STATE, CONV_DIM), lambda i, j: (0, i, 0, 0)),
                  _const_spec(w_dw.shape), vec, vec, vec, vec],
        out_specs=pl.BlockSpec((gb, tt, CONV_DIM), lambda i, j: (i, j, 0)),
        scratch_shapes=[pltpu.VMEM((gb, EXT_HEAD + tt, CONV_DIM), F32),
                        pltpu.VMEM((SUBLANES - 1, gb, EXT_HEAD + tt - SUBLANES, CONV_DIM), F32)],
        compiler_params=_params(("parallel", "arbitrary")),
        name="conv_module",
    )(glu, state, w_dw, b_dw, g_cn, b_cn, g_out)


def _pad_heads(w, lo):
    d = w.shape[-1]
    w = jnp.pad(w, [(0, 0)] * (w.ndim - 1) + [(lo, HEAD_PAD - lo - d)])
    return w.reshape(*w.shape[:-2], D_HEADS_PAD)


def _rope_tables(pos):
    inv_freq = ROPE_THETA ** (-jnp.arange(0, ROPE_DIM, 2, dtype=F32) / ROPE_DIM)
    inv_lanes = jnp.concatenate([inv_freq, inv_freq, jnp.zeros((LANES - ROPE_DIM,), F32)])
    ang = pos.astype(F32)[:, None] * inv_lanes[None, :]
    lane = jnp.arange(LANES)[None, :]
    sin = jnp.sin(ang)
    zero = jnp.zeros_like(sin)
    s1 = jnp.where((lane >= HALF_ROPE) & (lane < ROPE_DIM), sin, zero)
    s2 = jnp.where(lane < HALF_ROPE, -sin, zero)
    return jnp.cos(ang), s1, s2


def kernel(x_prompt, x_sample, cache_kv_latent, cache_k_rope, state_conv, page_table, g_ffn1, w1_gate, w1_up, w1_down, g_mix, w_in, g_q, w_q_b, g_kv, w_kv_b, w_dw, b_dw, g_cn, b_cn, g_out_attn, g_out_conv, w_out, g_ffn2, w2_gate, w2_up, w2_down, g_final):
    depth = g_ffn1.shape[0]
    assert depth == 1
    batch, seq, _ = x_prompt.shape
    db, t_new, _ = x_sample.shape
    n_pages = page_table.shape[1]
    past_len = n_pages * PAGE_SIZE
    assert seq % TOKEN_TILE == 0 and TOKEN_TILE % t_new == 0 and (db * t_new) % TOKEN_TILE == 0
    l = 0
    row = lambda v: v.reshape(1, -1)

    win = jnp.concatenate(
        [w_in[l][:, :KR_OFF + ROPE_DIM], jnp.zeros((D_MODEL, LANES - ROPE_DIM), F32),
         w_in[l][:, KR_OFF + ROPE_DIM:]], axis=1).astype(BF16)
    wq_heads = jnp.concatenate([w_q_b[l][..., QK_NOPE:], w_q_b[l][..., :QK_NOPE]], axis=-1)
    wq = _pad_heads(wq_heads, 0).astype(BF16)
    w_uk = w_kv_b[l][..., :QK_NOPE]
    w_uv = w_kv_b[l][..., QK_NOPE:]
    wk_exp = _pad_heads(w_uk, ROPE_DIM).astype(BF16)
    wv_t = w_uv.reshape(KV_RANK, D_ATTN).T.astype(BF16)
    wk_abs = jnp.pad(jnp.transpose(w_uk, (1, 2, 0)),
                     ((0, 0), (ROPE_DIM, HEAD_PAD - ROPE_DIM - QK_NOPE), (0, 0))).astype(BF16)
    eye = jnp.eye(N_HEADS, dtype=F32)
    wuv_bd = (jnp.transpose(w_uv, (1, 0, 2))[:, :, None, :] * eye[:, None, :, None])
    wuv_bd = wuv_bd.reshape(N_HEADS * KV_RANK, D_ATTN).astype(BF16)
    wo_attn = w_out[l][:D_ATTN].astype(BF16)
    wo_conv = w_out[l][D_ATTN:].astype(BF16)
    w1g, w1u, w1d = w1_gate[l].astype(BF16), w1_up[l].astype(BF16), w1_down[l].astype(BF16)
    w2g, w2u, w2d = w2_gate[l].astype(BF16), w2_up[l].astype(BF16), w2_down[l].astype(BF16)
    unused = jnp.zeros((SUBLANES, LANES), BF16)

    def stream(x, tables, conv_state, conv_batch, absorbed, attend):
        nb, nt = x.shape[0], x.shape[1]
        n = nb * nt
        x1 = _ffn(x.reshape(n, D_MODEL), row(g_ffn1[l]), w1g, w1u, w1d)
        wa, wb = (wk_abs, unused) if absorbed else (wk_exp, wv_t)
        q, ckv, kpe, glu, *extra = _inproj(x1, row(g_mix[l]), win, row(g_q[l]), wq, row(g_kv[l]),
                                           wa, wb, tables, absorbed)
        attn = attend(q, ckv, kpe, *extra)
        glu3 = glu.reshape(nb, nt, CONV_DIM)
        conv = _conv(glu3, conv_state, w_dw[l], row(b_dw[l]), row(g_cn[l]), row(b_cn[l]),
                     row(g_out_conv[l]), min(CONV_TILE, nt), conv_batch).reshape(n, CONV_DIM)
        y = _merge_ffn(x1, attn, conv, wuv_bd if absorbed else unused, row(g_out_attn[l]),
                       wo_attn, wo_conv, row(g_ffn2[l]), w2g, w2u, w2d, row(g_final), absorbed)
        new_conv = jnp.concatenate([conv_state[0], glu3], axis=1)[:, -CONV_STATE:]
        return y.reshape(nb, nt, D_MODEL), ckv, kpe, new_conv

    def attend_prompt(q, ckv, kpe, k, vt):
        return _flash(q, k, vt, batch, seq)

    y_p, ckv_p, kpe_p, cv_p = stream(
        x_prompt, _rope_tables(jnp.arange(seq)),
        jnp.zeros((1, batch, CONV_STATE, CONV_DIM), F32), 1, False, attend_prompt)

    cache_kr_t = jnp.swapaxes(cache_k_rope, 2, 3)

    def attend_sample(q, ckv, kpe, q_abs):
        o_lat = _paged_attention(
            page_table, q_abs.reshape(db, t_new, N_HEADS * KV_RANK),
            q.reshape(db, t_new, D_HEADS_PAD),
            ckv.reshape(db, t_new, KV_RANK), kpe.reshape(db, t_new, ROPE_DIM),
            cache_kv_latent, cache_kr_t)
        return o_lat.reshape(db * t_new, N_HEADS * KV_RANK)

    pos_s = jnp.tile(past_len + jnp.arange(t_new), TOKEN_TILE // t_new)
    y_s, ckv_s, kpe_s, cv_s = stream(x_sample, _rope_tables(pos_s), state_conv,
                                     CONV_BATCH, True, attend_sample)

    n_pg = seq // PAGE_SIZE
    return (y_p, y_s,
            ckv_p.reshape(1, batch, n_pg, PAGE_SIZE, KV_RANK),
            kpe_p.reshape(1, batch, n_pg, PAGE_SIZE, ROPE_DIM),
            cv_p[None],
            ckv_s.reshape(1, db, t_new, KV_RANK),
            kpe_s.reshape(1, db, t_new, ROPE_DIM),
            cv_s[None])
```

```python
import functools

import jax
import jax.numpy as jnp
from jax import lax
from jax.experimental import pallas as pl
from jax.experimental.pallas import tpu as pltpu

D_MODEL = 1024
N_HEADS = 8
QK_NOPE = 64
ROPE_DIM = 32
HALF_ROPE = ROPE_DIM // 2
V_DIM = 64
Q_RANK = 384
KV_RANK = 256
CONV_DIM = 512
CONV_WIDTH = 31
CONV_STATE = CONV_WIDTH - 1
D_ATTN = N_HEADS * V_DIM
D_FF = 2816
PAGE_SIZE = 128
ROPE_THETA = 10000.0
EPS = 1e-6
SM_SCALE = (QK_NOPE + ROPE_DIM) ** -0.5
LOG2_E = 1.4426950408889634
NEG_INF = -1e30

LANES = 128
SUBLANES = 8
HEAD_PAD = LANES
D_HEADS_PAD = N_HEADS * HEAD_PAD
KR_OFF = Q_RANK + KV_RANK
GA_OFF = KR_OFF + LANES
GB_OFF = GA_OFF + CONV_DIM
IN_COLS_PAD = GB_OFF + CONV_DIM
VMEM_LIMIT = 56 * 1024 * 1024

TOKEN_TILE = 512
ATTN_TILE = TOKEN_TILE
ONES_ROWS = 16
HEADS_AHEAD = 2
HEADS_PER_STEP = 8
CONV_TILE = 512
CONV_ROWS = 64
CONV_BATCH = 16
CHUNK_KEYS = 2048
SCORES_AHEAD = 3

BF16 = jnp.bfloat16
F32 = jnp.float32


def _dot(a, b):
    return jnp.dot(a, b, preferred_element_type=F32)


def _dot_nt(a, b):
    return lax.dot_general(a, b, (((1,), (1,)), ((), ())), preferred_element_type=F32)


def _rms(x, g, axis=-1):
    ms = jnp.mean(x * x, axis=axis, keepdims=True)
    return x * lax.rsqrt(ms + EPS) * g


def _rope_tile(blk, c, s1, s2):
    return (blk * c + pltpu.roll(blk, HALF_ROPE, 1) * s1
            + pltpu.roll(blk, LANES - HALF_ROPE, 1) * s2)


def _const_spec(shape):
    nd = len(shape)
    return pl.BlockSpec(shape, lambda *_: (0,) * nd, pipeline_mode=pl.Buffered(1))


def _params(sem):
    return pltpu.CompilerParams(dimension_semantics=sem, vmem_limit_bytes=VMEM_LIMIT)


def _half_ffn(x, g_ref, wg_ref, wu_ref, wd_ref):
    h = _rms(x, g_ref[...]).astype(BF16)
    a = _dot(h, wg_ref[...])
    a = a * jax.nn.sigmoid(a) * _dot(h, wu_ref[...])
    return x + 0.5 * _dot(a.astype(BF16), wd_ref[...])


def _ffn_kernel(x_ref, g_ref, wg_ref, wu_ref, wd_ref, o_ref):
    o_ref[...] = _half_ffn(x_ref[...], g_ref, wg_ref, wu_ref, wd_ref)


def _ffn_weight_specs():
    return [_const_spec((1, D_MODEL)), _const_spec((D_MODEL, D_FF)),
            _const_spec((D_MODEL, D_FF)), _const_spec((D_FF, D_MODEL))]


def _ffn(x, g, wg, wu, wd):
    n = x.shape[0]
    tm = TOKEN_TILE
    tok = pl.BlockSpec((tm, D_MODEL), lambda i: (i, 0))
    return pl.pallas_call(
        _ffn_kernel,
        out_shape=jax.ShapeDtypeStruct((n, D_MODEL), F32),
        grid=(n // tm,),
        in_specs=[tok] + _ffn_weight_specs(),
        out_specs=tok,
        compiler_params=_params(("parallel",)),
        name="ffn",
    )(x, g, wg, wu, wd)


def _merge_ffn_kernel(x_ref, a_ref, cv_ref, wuv_ref, ga_ref, woa_ref, woc_ref,
                      g_ref, wg_ref, wu_ref, wd_ref, gf_ref, o_ref, *, absorbed):
    if absorbed:
        attn = _dot(a_ref[...].astype(BF16), wuv_ref[...])
    else:
        attn = a_ref[...]
    an = _rms(attn, ga_ref[...]).astype(BF16)
    x = x_ref[...] + _dot(an, woa_ref[...]) + _dot(cv_ref[...].astype(BF16), woc_ref[...])
    o_ref[...] = _rms(_half_ffn(x, g_ref, wg_ref, wu_ref, wd_ref), gf_ref[...])


def _merge_ffn(x, attn, conv, wuv, ga, woa, woc, g, wg, wu, wd, gf, absorbed):
    n = x.shape[0]
    tm = TOKEN_TILE
    row = lambda w: pl.BlockSpec((tm, w), lambda i: (i, 0))
    return pl.pallas_call(
        functools.partial(_merge_ffn_kernel, absorbed=absorbed),
        out_shape=jax.ShapeDtypeStruct((n, D_MODEL), F32),
        grid=(n // tm,),
        in_specs=[row(D_MODEL), row(attn.shape[1]), row(CONV_DIM), _const_spec(wuv.shape),
                  _const_spec((1, D_ATTN)), _const_spec(woa.shape), _const_spec(woc.shape)]
                 + _ffn_weight_specs() + [_const_spec((1, D_MODEL))],
        out_specs=row(D_MODEL),
        compiler_params=_params(("parallel",)),
        name="merge_ffn_absorbed" if absorbed else "merge_ffn",
    )(x, attn, conv, wuv, ga, woa, woc, g, wg, wu, wd, gf)


def _inproj_kernel(x_ref, gmix_ref, win_ref, gq_ref, wq_ref, gkv_ref, wa_ref, wb_ref,
                   c_ref, s1_ref, s2_ref, q_ref, ckv_ref, kpe_ref, glu_ref, *extra, absorbed):
    x = x_ref[...]
    h = _rms(x, gmix_ref[...]).astype(BF16)
    u = _dot(h, win_ref[...])
    c, s1, s2 = c_ref[...], s1_ref[...], s2_ref[...]

    qn = _rms(u[:, :Q_RANK], gq_ref[...]).astype(BF16)
    q_raw = _dot(qn, wq_ref[...])
    ckv = _rms(u[:, Q_RANK:KR_OFF], gkv_ref[...])
    ckv_ref[...] = ckv
    ckv_b = ckv.astype(BF16)
    kpe = _rope_tile(u[:, KR_OFF:GA_OFF], c, s1, s2)
    if absorbed:
        kpe_ref[...] = kpe[:, :ROPE_DIM]
    else:
        kpe_t = kpe.T[:ROPE_DIM]
        for pg in range(kpe_ref.shape[0]):
            kpe_ref[pg] = kpe_t[:, pg * PAGE_SIZE:(pg + 1) * PAGE_SIZE]
    glu_ref[...] = u[:, GA_OFF:GB_OFF] * jax.nn.sigmoid(u[:, GB_OFF:IN_COLS_PAD])

    if absorbed:
        (qabs_ref,) = extra
    else:
        k_ref, vt_ref = extra
        k_nope = _dot(ckv_b, wa_ref[...])
        vt_ref[0] = _dot_nt(wb_ref[...], ckv_b).astype(BF16)
    q_scale = SM_SCALE if absorbed else SM_SCALE * LOG2_E
    for hd in range(N_HEADS):
        sl = slice(hd * HEAD_PAD, (hd + 1) * HEAD_PAD)
        qh = (_rope_tile(q_raw[:, sl], c, s1, s2) * q_scale).astype(BF16)
        q_ref[:, sl] = qh.astype(q_ref.dtype)
        if absorbed:
            qabs_ref[:, hd * KV_RANK:(hd + 1) * KV_RANK] = (
                _dot(qh, wa_ref[hd]).astype(BF16).astype(F32))
        else:
            k_ref[:, sl] = (k_nope[:, sl] + kpe).astype(BF16)


def _inproj(x, gmix, win, gq, wq, gkv, wa, wb, tables, absorbed):
    n = x.shape[0]
    tm = TOKEN_TILE
    row = lambda w: pl.BlockSpec((tm, w), lambda i: (i, 0))
    table_blocks = tables[0].shape[0] // tm
    table = pl.BlockSpec((tm, LANES), lambda i: (i % table_blocks, 0))
    out_shape = [
        jax.ShapeDtypeStruct((n, D_HEADS_PAD), F32 if absorbed else BF16),
        jax.ShapeDtypeStruct((n, KV_RANK), F32),
        jax.ShapeDtypeStruct((n, ROPE_DIM) if absorbed else (n // PAGE_SIZE, ROPE_DIM, PAGE_SIZE),
                             F32),
        jax.ShapeDtypeStruct((n, CONV_DIM), F32),
    ]
    kpe_spec = row(ROPE_DIM) if absorbed else pl.BlockSpec(
        (tm // PAGE_SIZE, ROPE_DIM, PAGE_SIZE), lambda i: (i, 0, 0))
    out_specs = [row(D_HEADS_PAD), row(KV_RANK), kpe_spec, row(CONV_DIM)]
    if absorbed:
        out_shape.append(jax.ShapeDtypeStruct((n, N_HEADS * KV_RANK), F32))
        out_specs.append(row(N_HEADS * KV_RANK))
    else:
        out_shape += [jax.ShapeDtypeStruct((n, D_HEADS_PAD), BF16),
                      jax.ShapeDtypeStruct((n // tm, D_ATTN, tm), BF16)]
        out_specs += [row(D_HEADS_PAD), pl.BlockSpec((1, D_ATTN, tm), lambda i: (i, 0, 0))]
    return pl.pallas_call(
        functools.partial(_inproj_kernel, absorbed=absorbed),
        out_shape=out_shape,
        grid=(n // tm,),
        in_specs=[row(D_MODEL), _const_spec((1, D_MODEL)), _const_spec(win.shape),
                  _const_spec((1, Q_RANK)), _const_spec(wq.shape), _const_spec((1, KV_RANK)),
                  _const_spec(wa.shape), _const_spec(wb.shape), table, table, table],
        out_specs=out_specs,
        compiler_params=_params(("parallel",)),
        name="inproj_absorbed" if absorbed else "inproj",
    )(x, gmix, win, gq, wq, gkv, wa, wb, *tables)


def _flash_kernel(q_ref, k_ref, vt_ref, o_ref, m_ref, acc_ref, s_ref):
    t = ATTN_TILE
    qi = pl.program_id(2)
    heads = range(HEADS_PER_STEP)
    q = [q_ref[:, h * HEAD_PAD:(h + 1) * HEAD_PAD] for h in heads]
    m_ref[...] = jnp.full(m_ref.shape, NEG_INF, F32)
    acc_ref[...] = jnp.zeros(acc_ref.shape, F32)

    def scores(j, h):
        k = k_ref[pl.ds(pl.multiple_of(j * t, t), t), h * HEAD_PAD:(h + 1) * HEAD_PAD]
        return _dot_nt(k, q[h])

    for h in range(HEADS_AHEAD):
        s_ref[h] = scores(0, h)

    def tile(j, masked):
        ahead = {}
        ones = jnp.ones((ONES_ROWS, t), BF16)
        for h in heads:
            nh = h + HEADS_AHEAD
            if nh < HEADS_PER_STEP:
                ahead[nh] = scores(j, nh)
            s = s_ref[h] if h < HEADS_AHEAD else ahead.pop(h)
            if nh >= HEADS_PER_STEP and not masked:
                s_ref[nh - HEADS_PER_STEP] = scores(j + 1, nh - HEADS_PER_STEP)
            vt = jnp.concatenate([vt_ref[j, h * V_DIM:(h + 1) * V_DIM, :], ones], axis=0)
            if masked:
                keys = lax.broadcasted_iota(jnp.int32, (t, t), 0)
                queries = lax.broadcasted_iota(jnp.int32, (t, t), 1)
                s = jnp.where(queries >= keys, s, NEG_INF)
            m = m_ref[h]
            m_new = jnp.maximum(m, jnp.max(s, axis=0, keepdims=True))
            p = jnp.exp2(s - m_new).astype(BF16)
            acc_ref[h] = jnp.exp2(m - m_new) * acc_ref[h] + _dot(vt, p)
            m_ref[h] = m_new

    def body(j, carry):
        tile(j, False)
        return carry

    lax.fori_loop(0, qi, body, 0)
    tile(qi, True)
    for h in heads:
        acc = acc_ref[h]
        o_ref[:, h * V_DIM:(h + 1) * V_DIM] = (acc[:V_DIM] / acc[V_DIM:V_DIM + 1]).T


def _flash(q, k, vt, batch, seq):
    t = ATTN_TILE
    nq = seq // t
    width = HEADS_PER_STEP * HEAD_PAD
    rows = HEADS_PER_STEP * V_DIM
    return pl.pallas_call(
        _flash_kernel,
        out_shape=jax.ShapeDtypeStruct((batch * seq, D_ATTN), F32),
        grid=(batch, N_HEADS // HEADS_PER_STEP, nq),
        in_specs=[pl.BlockSpec((t, width), lambda b, h, i: (b * nq + i, h)),
                  pl.BlockSpec((seq, width), lambda b, h, i: (b, h)),
                  pl.BlockSpec((nq, rows, t), lambda b, h, i: (b, h, 0))],
        out_specs=pl.BlockSpec((t, rows), lambda b, h, i: (b * nq + i, h)),
        scratch_shapes=[pltpu.VMEM((HEADS_PER_STEP, 1, t), F32),
                        pltpu.VMEM((HEADS_PER_STEP, V_DIM + ONES_ROWS, t), F32),
                        pltpu.VMEM((HEADS_AHEAD, t, t), F32)],
        compiler_params=_params(("parallel", "parallel", "arbitrary")),
        name="flash_prompt",
    )(q, k, vt)


def _paged_kernel(pt_ref, qa_ref, qp_ref, cn_ref, kn_ref, kv_hbm, kr_hbm, o_ref,
                  kvbuf, krbuf, sem, *, n_pages, n_new):
    b = pl.program_id(0)
    nb = pl.num_programs(0)
    slot = b % 2
    rows = n_new * N_HEADS

    def page_copies(bb, sl, j, off):
        page = pt_ref[bb * n_pages + j]
        return (pltpu.make_async_copy(kv_hbm.at[0, page],
                                      kvbuf.at[sl, pl.ds(off, PAGE_SIZE), :], sem.at[sl, 0]),
                pltpu.make_async_copy(kr_hbm.at[0, page], krbuf.at[sl, j], sem.at[sl, 1]))

    def wait_all(sl):
        pltpu.make_async_copy(kvbuf.at[sl], kvbuf.at[sl], sem.at[sl, 0]).wait()
        pltpu.make_async_copy(krbuf.at[sl], krbuf.at[sl], sem.at[sl, 1]).wait()

    @pl.when(b == 0)
    def _():
        def body(j, carry):
            for cp in page_copies(0, 0, j, pl.multiple_of(j * PAGE_SIZE, PAGE_SIZE)):
                cp.start()
            return carry
        lax.fori_loop(0, n_pages, body, 0)

    nxt = (b + 1) % nb
    for j in range(n_pages):
        for cp in page_copies(nxt, 1 - slot, j, j * PAGE_SIZE):
            cp.start()
    wait_all(slot)

    qa_all = qa_ref[0]
    qp_all = qp_ref[0]
    qa = jnp.concatenate([qa_all[:, h * KV_RANK:(h + 1) * KV_RANK] for h in range(N_HEADS)], axis=0)
    qp = jnp.concatenate([qp_all[:, h * HEAD_PAD:h * HEAD_PAD + ROPE_DIM]
                          for h in range(N_HEADS)], axis=0)

    cn = cn_ref[0]
    s = _dot_nt(qa, cn) + _dot_nt(qp, kn_ref[0])
    tq = lax.broadcasted_iota(jnp.int32, (rows, n_new), 0) % n_new
    tk = lax.broadcasted_iota(jnp.int32, (rows, n_new), 1)
    s = jnp.where(tq >= tk, s, NEG_INF)
    m = jnp.max(s, axis=-1, keepdims=True)
    p = jnp.exp(s - m)
    parts = [(m, jnp.sum(p, axis=-1, keepdims=True), _dot(p, cn))]

    n_chunks = n_pages * PAGE_SIZE // CHUNK_KEYS

    def chunk_kv(c):
        return kvbuf[slot, c * CHUNK_KEYS:(c + 1) * CHUNK_KEYS, :]

    def scores(c):
        pages = range(c * CHUNK_KEYS // PAGE_SIZE, (c + 1) * CHUNK_KEYS // PAGE_SIZE)
        krt = jnp.concatenate([krbuf[slot, p] for p in pages], axis=1)
        return _dot_nt(qa, chunk_kv(c)) + _dot(qp, krt)

    ahead = [scores(c) for c in range(min(SCORES_AHEAD, n_chunks))]
    for c in range(n_chunks):
        if c + SCORES_AHEAD < n_chunks:
            ahead.append(scores(c + SCORES_AHEAD))
        s = ahead[c]
        m = jnp.max(s, axis=-1, keepdims=True)
        p = jnp.exp(s - m)
        parts.append((m, jnp.sum(p, axis=-1, keepdims=True), _dot(p, chunk_kv(c))))

    m_all = functools.reduce(jnp.maximum, [m for m, _, _ in parts])
    l = jnp.zeros((rows, 1), F32)
    acc = jnp.zeros((rows, KV_RANK), F32)
    for m, l_c, acc_c in parts:
        w = jnp.exp(m - m_all)
        l = l + w * l_c
        acc = acc + w * acc_c
    o = acc / l
    for h in range(N_HEADS):
        o_ref[0, :, h * KV_RANK:(h + 1) * KV_RANK] = o[h * n_new:(h + 1) * n_new]

    @pl.when(b == nb - 1)
    def _():
        wait_all(1 - slot)


def _paged_attention(page_table, q_abs, q_rope, c_new, k_new, cache_kv, cache_kr_t):
    db, n_new = c_new.shape[0], c_new.shape[1]
    n_pages = page_table.shape[1]
    chunk_keys = n_pages * PAGE_SIZE
    assert chunk_keys % CHUNK_KEYS == 0 and n_new % SUBLANES == 0
    per_b = lambda w: pl.BlockSpec((1, n_new, w), lambda b, pt: (b, 0, 0))
    grid_spec = pltpu.PrefetchScalarGridSpec(
        num_scalar_prefetch=1,
        grid=(db,),
        in_specs=[per_b(N_HEADS * KV_RANK), per_b(D_HEADS_PAD), per_b(KV_RANK),
                  per_b(ROPE_DIM), pl.BlockSpec(memory_space=pl.ANY),
                  pl.BlockSpec(memory_space=pl.ANY)],
        out_specs=per_b(N_HEADS * KV_RANK),
        scratch_shapes=[pltpu.VMEM((2, chunk_keys, KV_RANK), F32),
                        pltpu.VMEM((2, n_pages, ROPE_DIM, PAGE_SIZE), F32),
                        pltpu.SemaphoreType.DMA((2, 2))],
    )
    return pl.pallas_call(
        functools.partial(_paged_kernel, n_pages=n_pages, n_new=n_new),
        out_shape=jax.ShapeDtypeStruct((db, n_new, N_HEADS * KV_RANK), F32),
        grid_spec=grid_spec,
        compiler_params=_params(("arbitrary",)),
        name="paged_attention",
    )(page_table.reshape(-1), q_abs, q_rope, c_new, k_new, cache_kv, cache_kr_t)


EXT_HEAD = 32


def _conv_kernel(g_ref, st_ref, w_ref, bdw_ref, gcn_ref, bcn_ref, gout_ref, o_ref, ext_ref, sh_ref,
                 *, tt):
    ti = pl.program_id(1)
    pad = EXT_HEAD - CONV_STATE

    @pl.when(ti == 0)
    def _():
        ext_ref[:, 0:pad, :] = jnp.zeros((g_ref.shape[0], pad, CONV_DIM), F32)
        ext_ref[:, pad:EXT_HEAD, :] = st_ref[...]

    @pl.when(ti > 0)
    def _():
        ext_ref[:, 0:EXT_HEAD, :] = ext_ref[:, tt:tt + EXT_HEAD, :]

    ext_ref[:, EXT_HEAD:EXT_HEAD + tt, :] = g_ref[...]
    rb = min(CONV_ROWS, tt)
    gb = g_ref.shape[0]

    span = tt + EXT_HEAD - SUBLANES
    for r in range(1, SUBLANES):
        sh_ref[r - 1, :, 0:span, :] = ext_ref[:, r:r + span, :]

    def block(i, carry):
        r0 = pl.multiple_of(i * rb, rb)
        acc = jnp.zeros((gb, rb, CONV_DIM), F32) + bdw_ref[...]
        for o in range(pad, EXT_HEAD + 1):
            a, r = divmod(o, SUBLANES)
            rows = pl.ds(r0 + SUBLANES * a, rb)
            src = ext_ref[:, rows, :] if r == 0 else sh_ref[r - 1, :, rows, :]
            acc = acc + src * w_ref[o - pad:o - pad + 1, :]
        o_ref[:, pl.ds(r0, rb), :] = acc
        return carry

    lax.fori_loop(0, tt // rb, block, 0)
    conv = o_ref[...]
    mu = jnp.mean(conv, axis=-1, keepdims=True)
    d = conv - mu
    var = jnp.mean(d * d, axis=-1, keepdims=True)
    y = d * lax.rsqrt(var + EPS) * gcn_ref[...] + bcn_ref[...]
    y = y * jax.nn.sigmoid(y)
    o_ref[...] = _rms(y, gout_ref[...])


def _conv(glu, state, w_dw, b_dw, g_cn, b_cn, g_out, tt, gb):
    b, t = glu.shape[0], glu.shape[1]
    vec = _const_spec((1, CONV_DIM))
    return pl.pallas_call(
        functools.partial(_conv_kernel, tt=tt),
        out_shape=jax.ShapeDtypeStruct((b, t, CONV_DIM), F32),
        grid=(b // gb, t // tt),
        in_specs=[pl.BlockSpec((gb, tt, CONV_DIM), lambda i, j: (i, j, 0)),
                  pl.BlockSpec((None, gb, CONV_STATE, CONV_DIM), lambda i, j: (0, i, 0, 0)),
                  _const_spec(w_dw.shape), vec, vec, vec, vec],
        out_specs=pl.BlockSpec((gb, tt, CONV_DIM), lambda i, j: (i, j, 0)),
        scratch_shapes=[pltpu.VMEM((gb, EXT_HEAD + tt, CONV_DIM), F32),
                        pltpu.VMEM((SUBLANES - 1, gb, EXT_HEAD + tt - SUBLANES, CONV_DIM), F32)],
        compiler_params=_params(("parallel", "arbitrary")),
        name="conv_module",
    )(glu, state, w_dw, b_dw, g_cn, b_cn, g_out)


def _pad_heads(w, lo):
    d = w.shape[-1]
    w = jnp.pad(w, [(0, 0)] * (w.ndim - 1) + [(lo, HEAD_PAD - lo - d)])
    return w.reshape(*w.shape[:-2], D_HEADS_PAD)


def _rope_tables(pos):
    inv_freq = ROPE_THETA ** (-jnp.arange(0, ROPE_DIM, 2, dtype=F32) / ROPE_DIM)
    ang = pos.astype(F32)[:, None] * inv_freq[None, :]
    cos, sin = jnp.cos(ang), jnp.sin(ang)
    rows = pos.shape[0]
    zeros = lambda w: jnp.zeros((rows, w), F32)
    c = jnp.concatenate([cos, cos, jnp.ones((rows, LANES - ROPE_DIM), F32)], axis=1)
    s1 = jnp.concatenate([zeros(HALF_ROPE), sin, zeros(LANES - ROPE_DIM)], axis=1)
    s2 = jnp.concatenate([-sin, zeros(LANES - HALF_ROPE)], axis=1)
    return c, s1, s2


def kernel(x_prompt, x_sample, cache_kv_latent, cache_k_rope, state_conv, page_table, g_ffn1, w1_gate, w1_up, w1_down, g_mix, w_in, g_q, w_q_b, g_kv, w_kv_b, w_dw, b_dw, g_cn, b_cn, g_out_attn, g_out_conv, w_out, g_ffn2, w2_gate, w2_up, w2_down, g_final):
    depth = g_ffn1.shape[0]
    assert depth == 1
    batch, seq, _ = x_prompt.shape
    db, t_new, _ = x_sample.shape
    n_pages = page_table.shape[1]
    past_len = n_pages * PAGE_SIZE
    assert seq % TOKEN_TILE == 0 and TOKEN_TILE % t_new == 0 and (db * t_new) % TOKEN_TILE == 0
    l = 0
    row = lambda v: v.reshape(1, -1)

    win = jnp.concatenate(
        [w_in[l][:, :KR_OFF + ROPE_DIM], jnp.zeros((D_MODEL, LANES - ROPE_DIM), F32),
         w_in[l][:, KR_OFF + ROPE_DIM:]], axis=1).astype(BF16)
    wq_heads = jnp.concatenate([w_q_b[l][..., QK_NOPE:], w_q_b[l][..., :QK_NOPE]], axis=-1)
    wq = _pad_heads(wq_heads, 0).astype(BF16)
    w_uk = w_kv_b[l][..., :QK_NOPE]
    w_uv = w_kv_b[l][..., QK_NOPE:]
    wk_exp = _pad_heads(w_uk, ROPE_DIM).astype(BF16)
    wv_t = w_uv.reshape(KV_RANK, D_ATTN).T.astype(BF16)
    wk_abs = jnp.pad(jnp.transpose(w_uk, (1, 2, 0)),
                     ((0, 0), (ROPE_DIM, HEAD_PAD - ROPE_DIM - QK_NOPE), (0, 0))).astype(BF16)
    eye = jnp.eye(N_HEADS, dtype=F32)
    wuv_bd = (jnp.transpose(w_uv, (1, 0, 2))[:, :, None, :] * eye[:, None, :, None])
    wuv_bd = wuv_bd.reshape(N_HEADS * KV_RANK, D_ATTN).astype(BF16)
    wo_attn = w_out[l][:D_ATTN].astype(BF16)
    wo_conv = w_out[l][D_ATTN:].astype(BF16)
    w1g, w1u, w1d = w1_gate[l].astype(BF16), w1_up[l].astype(BF16), w1_down[l].astype(BF16)
    w2g, w2u, w2d = w2_gate[l].astype(BF16), w2_up[l].astype(BF16), w2_down[l].astype(BF16)
    unused = jnp.zeros((SUBLANES, LANES), BF16)

    def stream(x, tables, conv_state, conv_batch, absorbed, attend):
        nb, nt = x.shape[0], x.shape[1]
        n = nb * nt
        x1 = _ffn(x.reshape(n, D_MODEL), row(g_ffn1[l]), w1g, w1u, w1d)
        wa, wb = (wk_abs, unused) if absorbed else (wk_exp, wv_t)
        q, ckv, kpe, glu, *extra = _inproj(x1, row(g_mix[l]), win, row(g_q[l]), wq, row(g_kv[l]),
                                           wa, wb, tables, absorbed)
        attn = attend(q, ckv, kpe, *extra)
        glu3 = glu.reshape(nb, nt, CONV_DIM)
        conv = _conv(glu3, conv_state, w_dw[l], row(b_dw[l]), row(g_cn[l]), row(b_cn[l]),
                     row(g_out_conv[l]), min(CONV_TILE, nt), conv_batch).reshape(n, CONV_DIM)
        y = _merge_ffn(x1, attn, conv, wuv_bd if absorbed else unused, row(g_out_attn[l]),
                       wo_attn, wo_conv, row(g_ffn2[l]), w2g, w2u, w2d, row(g_final), absorbed)
        new_conv = jnp.concatenate([conv_state[0], glu3], axis=1)[:, -CONV_STATE:]
        return y.reshape(nb, nt, D_MODEL), ckv, kpe, new_conv

    def attend_prompt(q, ckv, kpe, k, vt):
        return _flash(q, k, vt, batch, seq)

    y_p, ckv_p, kpe_p, cv_p = stream(
        x_prompt, _rope_tables(jnp.arange(seq)),
        jnp.zeros((1, batch, CONV_STATE, CONV_DIM), F32), 1, False, attend_prompt)

    cache_kr_t = jnp.swapaxes(cache_k_rope, 2, 3)

    def attend_sample(q, ckv, kpe, q_abs):
        o_lat = _paged_attention(
            page_table, q_abs.reshape(db, t_new, N_HEADS * KV_RANK),
            q.reshape(db, t_new, D_HEADS_PAD),
            ckv.reshape(db, t_new, KV_RANK), kpe.reshape(db, t_new, ROPE_DIM),
            cache_kv_latent, cache_kr_t)
        return o_lat.reshape(db * t_new, N_HEADS * KV_RANK)

    pos_s = jnp.tile(past_len + jnp.arange(t_new), TOKEN_TILE // t_new)
    y_s, ckv_s, kpe_s, cv_s = stream(x_sample, _rope_tables(pos_s), state_conv,
                                     CONV_BATCH, True, attend_sample)

    n_pg = seq // PAGE_SIZE
    return (y_p, y_s,
            ckv_p.reshape(1, batch, n_pg, PAGE_SIZE, KV_RANK),
            jnp.swapaxes(kpe_p.reshape(1, batch, n_pg, ROPE_DIM, PAGE_SIZE), 3, 4),
            cv_p[None],
            ckv_s.reshape(1, db, t_new, KV_RANK),
            kpe_s.reshape(1, db, t_new, ROPE_DIM),
            cv_s[None])
```

```python
import functools

import jax
import jax.numpy as jnp
from jax import lax
from jax.experimental import pallas as pl
from jax.experimental.pallas import tpu as pltpu

D_MODEL = 1024
N_HEADS = 8
QK_NOPE = 64
ROPE_DIM = 32
HALF_ROPE = ROPE_DIM // 2
V_DIM = 64
Q_RANK = 384
KV_RANK = 256
CONV_DIM = 512
CONV_WIDTH = 31
CONV_STATE = CONV_WIDTH - 1
D_ATTN = N_HEADS * V_DIM
D_FF = 2816
PAGE_SIZE = 128
ROPE_THETA = 10000.0
EPS = 1e-6
SM_SCALE = (QK_NOPE + ROPE_DIM) ** -0.5
LOG2_E = 1.4426950408889634
NEG_INF = -1e30

LANES = 128
SUBLANES = 8
HEAD_PAD = LANES
D_HEADS_PAD = N_HEADS * HEAD_PAD
KR_OFF = Q_RANK + KV_RANK
GA_OFF = KR_OFF + LANES
GB_OFF = GA_OFF + CONV_DIM
IN_COLS_PAD = GB_OFF + CONV_DIM
VMEM_LIMIT = 56 * 1024 * 1024

TOKEN_TILE = 512
ATTN_TILE = TOKEN_TILE
ONES_ROWS = 16
HEADS_AHEAD = 2
HEADS_PER_STEP = 8
CONV_TILE = 512
CONV_ROWS = 128
CONV_BATCH = 16
CHUNK_KEYS = 2048
SCORES_AHEAD = 3

BF16 = jnp.bfloat16
F32 = jnp.float32


def _dot(a, b):
    return jnp.dot(a, b, preferred_element_type=F32)


def _dot_nt(a, b):
    return lax.dot_general(a, b, (((1,), (1,)), ((), ())), preferred_element_type=F32)


def _rms(x, g, axis=-1):
    ms = jnp.mean(x * x, axis=axis, keepdims=True)
    return x * lax.rsqrt(ms + EPS) * g


def _rope_tile(blk, c, s1, s2):
    return (blk * c + pltpu.roll(blk, HALF_ROPE, 1) * s1
            + pltpu.roll(blk, LANES - HALF_ROPE, 1) * s2)


def _const_spec(shape):
    nd = len(shape)
    return pl.BlockSpec(shape, lambda *_: (0,) * nd, pipeline_mode=pl.Buffered(1))


def _params(sem):
    return pltpu.CompilerParams(dimension_semantics=sem, vmem_limit_bytes=VMEM_LIMIT)


def _half_ffn(x, g_ref, wg_ref, wu_ref, wd_ref):
    h = _rms(x, g_ref[...]).astype(BF16)
    a = _dot(h, wg_ref[...])
    a = a * jax.nn.sigmoid(a) * _dot(h, wu_ref[...])
    return x + 0.5 * _dot(a.astype(BF16), wd_ref[...])


def _ffn_kernel(x_ref, g_ref, wg_ref, wu_ref, wd_ref, o_ref):
    o_ref[...] = _half_ffn(x_ref[...], g_ref, wg_ref, wu_ref, wd_ref)


def _ffn_weight_specs():
    return [_const_spec((1, D_MODEL)), _const_spec((D_MODEL, D_FF)),
            _const_spec((D_MODEL, D_FF)), _const_spec((D_FF, D_MODEL))]


def _ffn(x, g, wg, wu, wd):
    n = x.shape[0]
    tm = TOKEN_TILE
    tok = pl.BlockSpec((tm, D_MODEL), lambda i: (i, 0))
    return pl.pallas_call(
        _ffn_kernel,
        out_shape=jax.ShapeDtypeStruct((n, D_MODEL), F32),
        grid=(n // tm,),
        in_specs=[tok] + _ffn_weight_specs(),
        out_specs=tok,
        compiler_params=_params(("parallel",)),
        name="ffn",
    )(x, g, wg, wu, wd)


def _merge_ffn_kernel(x_ref, a_ref, cv_ref, wuv_ref, ga_ref, woa_ref, woc_ref,
                      g_ref, wg_ref, wu_ref, wd_ref, gf_ref, o_ref, *, absorbed):
    if absorbed:
        attn = _dot(a_ref[...].astype(BF16), wuv_ref[...])
    else:
        attn = a_ref[...]
    an = _rms(attn, ga_ref[...]).astype(BF16)
    x = x_ref[...] + _dot(an, woa_ref[...]) + _dot(cv_ref[...].astype(BF16), woc_ref[...])
    o_ref[...] = _rms(_half_ffn(x, g_ref, wg_ref, wu_ref, wd_ref), gf_ref[...])


def _merge_ffn(x, attn, conv, wuv, ga, woa, woc, g, wg, wu, wd, gf, absorbed):
    n = x.shape[0]
    tm = TOKEN_TILE
    row = lambda w: pl.BlockSpec((tm, w), lambda i: (i, 0))
    return pl.pallas_call(
        functools.partial(_merge_ffn_kernel, absorbed=absorbed),
        out_shape=jax.ShapeDtypeStruct((n, D_MODEL), F32),
        grid=(n // tm,),
        in_specs=[row(D_MODEL), row(attn.shape[1]), row(CONV_DIM), _const_spec(wuv.shape),
                  _const_spec((1, D_ATTN)), _const_spec(woa.shape), _const_spec(woc.shape)]
                 + _ffn_weight_specs() + [_const_spec((1, D_MODEL))],
        out_specs=row(D_MODEL),
        compiler_params=_params(("parallel",)),
        name="merge_ffn_absorbed" if absorbed else "merge_ffn",
    )(x, attn, conv, wuv, ga, woa, woc, g, wg, wu, wd, gf)


def _inproj_kernel(x_ref, gmix_ref, win_ref, gq_ref, wq_ref, gkv_ref, wa_ref, wb_ref,
                   c_ref, s1_ref, s2_ref, q_ref, ckv_ref, kpe_ref, glu_ref, *extra, absorbed):
    x = x_ref[...]
    h = _rms(x, gmix_ref[...]).astype(BF16)
    u = _dot(h, win_ref[...])
    c, s1, s2 = c_ref[...], s1_ref[...], s2_ref[...]

    qn = _rms(u[:, :Q_RANK], gq_ref[...]).astype(BF16)
    q_raw = _dot(qn, wq_ref[...])
    ckv = _rms(u[:, Q_RANK:KR_OFF], gkv_ref[...])
    ckv_ref[...] = ckv
    ckv_b = ckv.astype(BF16)
    kpe = _rope_tile(u[:, KR_OFF:GA_OFF], c, s1, s2)
    if absorbed:
        kpe_ref[...] = kpe[:, :ROPE_DIM]
    else:
        kpe_t = kpe.T[:ROPE_DIM]
        for pg in range(kpe_ref.shape[0]):
            kpe_ref[pg] = kpe_t[:, pg * PAGE_SIZE:(pg + 1) * PAGE_SIZE]
    glu_ref[...] = u[:, GA_OFF:GB_OFF] * jax.nn.sigmoid(u[:, GB_OFF:IN_COLS_PAD])

    if absorbed:
        (qabs_ref,) = extra
    else:
        k_ref, vt_ref = extra
        k_nope = _dot(ckv_b, wa_ref[...])
        vt_ref[0] = _dot_nt(wb_ref[...], ckv_b).astype(BF16)
    q_scale = SM_SCALE if absorbed else SM_SCALE * LOG2_E
    for hd in range(N_HEADS):
        sl = slice(hd * HEAD_PAD, (hd + 1) * HEAD_PAD)
        qh = (_rope_tile(q_raw[:, sl], c, s1, s2) * q_scale).astype(BF16)
        q_ref[:, sl] = qh.astype(q_ref.dtype)
        if absorbed:
            qabs_ref[:, hd * KV_RANK:(hd + 1) * KV_RANK] = (
                _dot(qh, wa_ref[hd]).astype(BF16).astype(F32))
        else:
            k_ref[:, sl] = (k_nope[:, sl] + kpe).astype(BF16)


def _inproj(x, gmix, win, gq, wq, gkv, wa, wb, tables, absorbed):
    n = x.shape[0]
    tm = TOKEN_TILE
    row = lambda w: pl.BlockSpec((tm, w), lambda i: (i, 0))
    table_blocks = tables[0].shape[0] // tm
    table = pl.BlockSpec((tm, LANES), lambda i: (i % table_blocks, 0))
    out_shape = [
        jax.ShapeDtypeStruct((n, D_HEADS_PAD), F32 if absorbed else BF16),
        jax.ShapeDtypeStruct((n, KV_RANK), F32),
        jax.ShapeDtypeStruct((n, ROPE_DIM) if absorbed else (n // PAGE_SIZE, ROPE_DIM, PAGE_SIZE),
                             F32),
        jax.ShapeDtypeStruct((n, CONV_DIM), F32),
    ]
    kpe_spec = row(ROPE_DIM) if absorbed else pl.BlockSpec(
        (tm // PAGE_SIZE, ROPE_DIM, PAGE_SIZE), lambda i: (i, 0, 0))
    out_specs = [row(D_HEADS_PAD), row(KV_RANK), kpe_spec, row(CONV_DIM)]
    if absorbed:
        out_shape.append(jax.ShapeDtypeStruct((n, N_HEADS * KV_RANK), F32))
        out_specs.append(row(N_HEADS * KV_RANK))
    else:
        out_shape += [jax.ShapeDtypeStruct((n, D_HEADS_PAD), BF16),
                      jax.ShapeDtypeStruct((n // tm, D_ATTN, tm), BF16)]
        out_specs += [row(D_HEADS_PAD), pl.BlockSpec((1, D_ATTN, tm), lambda i: (i, 0, 0))]
    return pl.pallas_call(
        functools.partial(_inproj_kernel, absorbed=absorbed),
        out_shape=out_shape,
        grid=(n // tm,),
        in_specs=[row(D_MODEL), _const_spec((1, D_MODEL)), _const_spec(win.shape),
                  _const_spec((1, Q_RANK)), _const_spec(wq.shape), _const_spec((1, KV_RANK)),
                  _const_spec(wa.shape), _const_spec(wb.shape), table, table, table],
        out_specs=out_specs,
        compiler_params=_params(("parallel",)),
        name="inproj_absorbed" if absorbed else "inproj",
    )(x, gmix, win, gq, wq, gkv, wa, wb, *tables)


def _flash_kernel(q_ref, k_ref, vt_ref, o_ref, m_ref, acc_ref):
    t = ATTN_TILE
    half = t // 2
    qi = pl.program_id(2)
    heads = range(HEADS_PER_STEP)
    m_ref[...] = jnp.full(m_ref.shape, NEG_INF, F32)
    acc_ref[...] = jnp.zeros(acc_ref.shape, F32)

    def scores(j, h, n_keys, queries):
        cols = slice(h * HEAD_PAD, (h + 1) * HEAD_PAD)
        k = k_ref[pl.ds(pl.multiple_of(j * t, t), n_keys), cols]
        return _dot_nt(k, q_ref[queries, cols])

    def update(j, h, s, n_keys, queries):
        vt = jnp.concatenate([vt_ref[j, h * V_DIM:(h + 1) * V_DIM, :n_keys],
                              jnp.ones((ONES_ROWS, n_keys), BF16)], axis=0)
        m = m_ref[h, :, queries]
        m_new = jnp.maximum(m, jnp.max(s, axis=0, keepdims=True))
        p = jnp.exp2(s - m_new).astype(BF16)
        acc_ref[h, :, queries] = jnp.exp2(m - m_new) * acc_ref[h, :, queries] + _dot(vt, p)
        m_ref[h, :, queries] = m_new

    def run(units, score_fn, update_fn):
        ahead = [score_fn(u) for u in units[:HEADS_AHEAD]]
        for i, u in enumerate(units):
            if i + HEADS_AHEAD < len(units):
                ahead.append(score_fn(units[i + HEADS_AHEAD]))
            update_fn(u, ahead[i])

    def full_tile(j, carry):
        everything = slice(0, t)
        run(list(heads), lambda h: scores(j, h, t, everything),
            lambda h, s: update(j, h, s, t, everything))
        return carry

    lax.fori_loop(0, qi, full_tile, 0)

    def diag_scores(u):
        h, part = u
        return scores(qi, h, (part + 1) * half, slice(part * half, (part + 1) * half))

    def diag_update(u, s):
        h, part = u
        n_keys = (part + 1) * half
        key_pos = lax.broadcasted_iota(jnp.int32, (n_keys, half), 0)
        query_pos = lax.broadcasted_iota(jnp.int32, (n_keys, half), 1) + part * half
        s = jnp.where(query_pos >= key_pos, s, NEG_INF)
        update(qi, h, s, n_keys, slice(part * half, (part + 1) * half))

    run([(h, part) for h in heads for part in range(2)], diag_scores, diag_update)
    for h in heads:
        acc = acc_ref[h]
        o_ref[:, h * V_DIM:(h + 1) * V_DIM] = (acc[:V_DIM] / acc[V_DIM:V_DIM + 1]).T


def _flash(q, k, vt, batch, seq):
    t = ATTN_TILE
    nq = seq // t
    width = HEADS_PER_STEP * HEAD_PAD
    rows = HEADS_PER_STEP * V_DIM
    return pl.pallas_call(
        _flash_kernel,
        out_shape=jax.ShapeDtypeStruct((batch * seq, D_ATTN), F32),
        grid=(batch, N_HEADS // HEADS_PER_STEP, nq),
        in_specs=[pl.BlockSpec((t, width), lambda b, h, i: (b * nq + i, h)),
                  pl.BlockSpec((seq, width), lambda b, h, i: (b, h)),
                  pl.BlockSpec((nq, rows, t), lambda b, h, i: (b, h, 0))],
        out_specs=pl.BlockSpec((t, rows), lambda b, h, i: (b * nq + i, h)),
        scratch_shapes=[pltpu.VMEM((HEADS_PER_STEP, 1, t), F32),
                        pltpu.VMEM((HEADS_PER_STEP, V_DIM + ONES_ROWS, t), F32)],
        compiler_params=_params(("parallel", "parallel", "arbitrary")),
        name="flash_prompt",
    )(q, k, vt)


def _paged_kernel(pt_ref, qa_ref, qp_ref, cn_ref, kn_ref, kv_hbm, kr_hbm, o_ref,
                  kvbuf, krbuf, sem, *, n_pages, n_new):
    b = pl.program_id(0)
    nb = pl.num_programs(0)
    slot = b % 2
    rows = n_new * N_HEADS

    def page_copies(bb, sl, j, off):
        page = pt_ref[bb * n_pages + j]
        return (pltpu.make_async_copy(kv_hbm.at[0, page],
                                      kvbuf.at[sl, pl.ds(off, PAGE_SIZE), :], sem.at[sl, 0]),
                pltpu.make_async_copy(kr_hbm.at[0, page], krbuf.at[sl, j], sem.at[sl, 1]))

    def wait_all(sl):
        pltpu.make_async_copy(kvbuf.at[sl], kvbuf.at[sl], sem.at[sl, 0]).wait()
        pltpu.make_async_copy(krbuf.at[sl], krbuf.at[sl], sem.at[sl, 1]).wait()

    @pl.when(b == 0)
    def _():
        def body(j, carry):
            for cp in page_copies(0, 0, j, pl.multiple_of(j * PAGE_SIZE, PAGE_SIZE)):
                cp.start()
            return carry
        lax.fori_loop(0, n_pages, body, 0)

    nxt = (b + 1) % nb
    for j in range(n_pages):
        for cp in page_copies(nxt, 1 - slot, j, j * PAGE_SIZE):
            cp.start()
    wait_all(slot)

    qa_all = qa_ref[0]
    qp_all = qp_ref[0]
    qa = jnp.concatenate([qa_all[:, h * KV_RANK:(h + 1) * KV_RANK] for h in range(N_HEADS)], axis=0)
    qp = jnp.concatenate([qp_all[:, h * HEAD_PAD:h * HEAD_PAD + ROPE_DIM]
                          for h in range(N_HEADS)], axis=0)

    cn = cn_ref[0]
    s = _dot_nt(qa, cn) + _dot_nt(qp, kn_ref[0])
    tq = lax.broadcasted_iota(jnp.int32, (rows, n_new), 0) % n_new
    tk = lax.broadcasted_iota(jnp.int32, (rows, n_new), 1)
    s = jnp.where(tq >= tk, s, NEG_INF)
    m = jnp.max(s, axis=-1, keepdims=True)
    p = jnp.exp(s - m)
    parts = [(m, jnp.sum(p, axis=-1, keepdims=True), _dot(p, cn))]

    n_chunks = n_pages * PAGE_SIZE // CHUNK_KEYS

    def chunk_kv(c):
        return kvbuf[slot, c * CHUNK_KEYS:(c + 1) * CHUNK_KEYS, :]

    def scores(c):
        pages = range(c * CHUNK_KEYS // PAGE_SIZE, (c + 1) * CHUNK_KEYS // PAGE_SIZE)
        krt = jnp.concatenate([krbuf[slot, p] for p in pages], axis=1)
        return _dot_nt(qa, chunk_kv(c)) + _dot(qp, krt)

    ahead = [scores(c) for c in range(min(SCORES_AHEAD, n_chunks))]
    for c in range(n_chunks):
        if c + SCORES_AHEAD < n_chunks:
            ahead.append(scores(c + SCORES_AHEAD))
        s = ahead[c]
        m = jnp.max(s, axis=-1, keepdims=True)
        p = jnp.exp(s - m)
        parts.append((m, jnp.sum(p, axis=-1, keepdims=True), _dot(p, chunk_kv(c))))

    m_all = functools.reduce(jnp.maximum, [m for m, _, _ in parts])
    l = jnp.zeros((rows, 1), F32)
    acc = jnp.zeros((rows, KV_RANK), F32)
    for m, l_c, acc_c in parts:
        w = jnp.exp(m - m_all)
        l = l + w * l_c
        acc = acc + w * acc_c
    o = acc / l
    for h in range(N_HEADS):
        o_ref[0, :, h * KV_RANK:(h + 1) * KV_RANK] = o[h * n_new:(h + 1) * n_new]

    @pl.when(b == nb - 1)
    def _():
        wait_all(1 - slot)


def _paged_attention(page_table, q_abs, q_rope, c_new, k_new, cache_kv, cache_kr_t):
    db, n_new = c_new.shape[0], c_new.shape[1]
    n_pages = page_table.shape[1]
    chunk_keys = n_pages * PAGE_SIZE
    assert chunk_keys % CHUNK_KEYS == 0 and n_new % SUBLANES == 0
    per_b = lambda w: pl.BlockSpec((1, n_new, w), lambda b, pt: (b, 0, 0))
    grid_spec = pltpu.PrefetchScalarGridSpec(
        num_scalar_prefetch=1,
        grid=(db,),
        in_specs=[per_b(N_HEADS * KV_RANK), per_b(D_HEADS_PAD), per_b(KV_RANK),
                  per_b(ROPE_DIM), pl.BlockSpec(memory_space=pl.ANY),
                  pl.BlockSpec(memory_space=pl.ANY)],
        out_specs=per_b(N_HEADS * KV_RANK),
        scratch_shapes=[pltpu.VMEM((2, chunk_keys, KV_RANK), F32),
                        pltpu.VMEM((2, n_pages, ROPE_DIM, PAGE_SIZE), F32),
                        pltpu.SemaphoreType.DMA((2, 2))],
    )
    return pl.pallas_call(
        functools.partial(_paged_kernel, n_pages=n_pages, n_new=n_new),
        out_shape=jax.ShapeDtypeStruct((db, n_new, N_HEADS * KV_RANK), F32),
        grid_spec=grid_spec,
        compiler_params=_params(("arbitrary",)),
        name="paged_attention",
    )(page_table.reshape(-1), q_abs, q_rope, c_new, k_new, cache_kv, cache_kr_t)


EXT_HEAD = 32


def _conv_kernel(g_ref, st_ref, w_ref, bdw_ref, gcn_ref, bcn_ref, gout_ref, o_ref, ext_ref, sh_ref,
                 *, tt):
    ti = pl.program_id(1)
    pad = EXT_HEAD - CONV_STATE
    rb = min(CONV_ROWS, tt)
    gb = g_ref.shape[0]
    lane_tiles = [slice(c * LANES, (c + 1) * LANES) for c in range(CONV_DIM // LANES)]

    @pl.when(ti == 0)
    def _():
        ext_ref[:, :, 0:pad, :] = jnp.zeros((gb, len(lane_tiles), pad, LANES), F32)
        for c, cs in enumerate(lane_tiles):
            ext_ref[:, c, pad:EXT_HEAD, :] = st_ref[:, :, cs]

    @pl.when(ti > 0)
    def _():
        ext_ref[:, :, 0:EXT_HEAD, :] = ext_ref[:, :, tt:tt + EXT_HEAD, :]

    for c, cs in enumerate(lane_tiles):
        ext_ref[:, c, EXT_HEAD:EXT_HEAD + tt, :] = g_ref[:, :, cs]

    span = tt + EXT_HEAD - SUBLANES
    for r in range(1, SUBLANES):
        sh_ref[r - 1, :, :, 0:span, :] = ext_ref[:, :, r:r + span, :]

    for c, cs in enumerate(lane_tiles):
        taps = [w_ref[k:k + 1, cs] for k in range(CONV_WIDTH)]
        bias = bdw_ref[:, cs]

        def block(i, carry, c=c, cs=cs, taps=taps, bias=bias):
            r0 = pl.multiple_of(i * rb, rb)
            acc = jnp.zeros((gb, rb, LANES), F32) + bias
            for o in range(pad, EXT_HEAD + 1):
                a, r = divmod(o, SUBLANES)
                rows = pl.ds(r0 + SUBLANES * a, rb)
                src = ext_ref[:, c, rows, :] if r == 0 else sh_ref[r - 1, :, c, rows, :]
                acc = acc + src * taps[o - pad]
            o_ref[:, pl.ds(r0, rb), cs] = acc
            return carry

        lax.fori_loop(0, tt // rb, block, 0)
    conv = o_ref[...]
    mu = jnp.mean(conv, axis=-1, keepdims=True)
    d = conv - mu
    var = jnp.mean(d * d, axis=-1, keepdims=True)
    y = d * lax.rsqrt(var + EPS) * gcn_ref[...] + bcn_ref[...]
    y = y * jax.nn.sigmoid(y)
    o_ref[...] = _rms(y, gout_ref[...])


def _conv(glu, state, w_dw, b_dw, g_cn, b_cn, g_out, tt, gb):
    b, t = glu.shape[0], glu.shape[1]
    vec = _const_spec((1, CONV_DIM))
    return pl.pallas_call(
        functools.partial(_conv_kernel, tt=tt),
        out_shape=jax.ShapeDtypeStruct((b, t, CONV_DIM), F32),
        grid=(b // gb, t // tt),
        in_specs=[pl.BlockSpec((gb, tt, CONV_DIM), lambda i, j: (i, j, 0)),
                  pl.BlockSpec((None, gb, CONV_STATE, CONV_DIM), lambda i, j: (0, i, 0, 0)),
                  _const_spec(w_dw.shape), vec, vec, vec, vec],
        out_specs=pl.BlockSpec((gb, tt, CONV_DIM), lambda i, j: (i, j, 0)),
        scratch_shapes=[pltpu.VMEM((gb, CONV_DIM // LANES, EXT_HEAD + tt, LANES), F32),
                        pltpu.VMEM((SUBLANES - 1, gb, CONV_DIM // LANES,
                                    EXT_HEAD + tt - SUBLANES, LANES), F32)],
        compiler_params=_params(("parallel", "arbitrary")),
        name="conv_module",
    )(glu, state, w_dw, b_dw, g_cn, b_cn, g_out)


def _pad_heads(w, lo):
    d = w.shape[-1]
    w = jnp.pad(w, [(0, 0)] * (w.ndim - 1) + [(lo, HEAD_PAD - lo - d)])
    return w.reshape(*w.shape[:-2], D_HEADS_PAD)


def _rope_tables(pos):
    inv_freq = ROPE_THETA ** (-jnp.arange(0, ROPE_DIM, 2, dtype=F32) / ROPE_DIM)
    ang = pos.astype(F32)[:, None] * inv_freq[None, :]
    cos, sin = jnp.cos(ang), jnp.sin(ang)
    rows = pos.shape[0]
    zeros = lambda w: jnp.zeros((rows, w), F32)
    c = jnp.concatenate([cos, cos, jnp.ones((rows, LANES - ROPE_DIM), F32)], axis=1)
    s1 = jnp.concatenate([zeros(HALF_ROPE), sin, zeros(LANES - ROPE_DIM)], axis=1)
    s2 = jnp.concatenate([-sin, zeros(LANES - HALF_ROPE)], axis=1)
    return c, s1, s2


def kernel(x_prompt, x_sample, cache_kv_latent, cache_k_rope, state_conv, page_table, g_ffn1, w1_gate, w1_up, w1_down, g_mix, w_in, g_q, w_q_b, g_kv, w_kv_b, w_dw, b_dw, g_cn, b_cn, g_out_attn, g_out_conv, w_out, g_ffn2, w2_gate, w2_up, w2_down, g_final):
    depth = g_ffn1.shape[0]
    assert depth == 1
    batch, seq, _ = x_prompt.shape
    db, t_new, _ = x_sample.shape
    n_pages = page_table.shape[1]
    past_len = n_pages * PAGE_SIZE
    assert seq % TOKEN_TILE == 0 and TOKEN_TILE % t_new == 0 and (db * t_new) % TOKEN_TILE == 0
    l = 0
    row = lambda v: v.reshape(1, -1)

    win = jnp.concatenate(
        [w_in[l][:, :KR_OFF + ROPE_DIM], jnp.zeros((D_MODEL, LANES - ROPE_DIM), F32),
         w_in[l][:, KR_OFF + ROPE_DIM:]], axis=1).astype(BF16)
    wq_heads = jnp.concatenate([w_q_b[l][..., QK_NOPE:], w_q_b[l][..., :QK_NOPE]], axis=-1)
    wq = _pad_heads(wq_heads, 0).astype(BF16)
    w_uk = w_kv_b[l][..., :QK_NOPE]
    w_uv = w_kv_b[l][..., QK_NOPE:]
    wk_exp = _pad_heads(w_uk, ROPE_DIM).astype(BF16)
    wv_t = w_uv.reshape(KV_RANK, D_ATTN).T.astype(BF16)
    wk_abs = jnp.pad(jnp.transpose(w_uk, (1, 2, 0)),
                     ((0, 0), (ROPE_DIM, HEAD_PAD - ROPE_DIM - QK_NOPE), (0, 0))).astype(BF16)
    eye = jnp.eye(N_HEADS, dtype=F32)
    wuv_bd = (jnp.transpose(w_uv, (1, 0, 2))[:, :, None, :] * eye[:, None, :, None])
    wuv_bd = wuv_bd.reshape(N_HEADS * KV_RANK, D_ATTN).astype(BF16)
    wo_attn = w_out[l][:D_ATTN].astype(BF16)
    wo_conv = w_out[l][D_ATTN:].astype(BF16)
    w1g, w1u, w1d = w1_gate[l].astype(BF16), w1_up[l].astype(BF16), w1_down[l].astype(BF16)
    w2g, w2u, w2d = w2_gate[l].astype(BF16), w2_up[l].astype(BF16), w2_down[l].astype(BF16)
    unused = jnp.zeros((SUBLANES, LANES), BF16)

    def stream(x, tables, conv_state, conv_batch, absorbed, attend):
        nb, nt = x.shape[0], x.shape[1]
        n = nb * nt
        x1 = _ffn(x.reshape(n, D_MODEL), row(g_ffn1[l]), w1g, w1u, w1d)
        wa, wb = (wk_abs, unused) if absorbed else (wk_exp, wv_t)
        q, ckv, kpe, glu, *extra = _inproj(x1, row(g_mix[l]), win, row(g_q[l]), wq, row(g_kv[l]),
                                           wa, wb, tables, absorbed)
        attn = attend(q, ckv, kpe, *extra)
        glu3 = glu.reshape(nb, nt, CONV_DIM)
        conv = _conv(glu3, conv_state, w_dw[l], row(b_dw[l]), row(g_cn[l]), row(b_cn[l]),
                     row(g_out_conv[l]), min(CONV_TILE, nt), conv_batch).reshape(n, CONV_DIM)
        y = _merge_ffn(x1, attn, conv, wuv_bd if absorbed else unused, row(g_out_attn[l]),
                       wo_attn, wo_conv, row(g_ffn2[l]), w2g, w2u, w2d, row(g_final), absorbed)
        new_conv = jnp.concatenate([conv_state[0], glu3], axis=1)[:, -CONV_STATE:]
        return y.reshape(nb, nt, D_MODEL), ckv, kpe, new_conv

    def attend_prompt(q, ckv, kpe, k, vt):
        return _flash(q, k, vt, batch, seq)

    y_p, ckv_p, kpe_p, cv_p = stream(
        x_prompt, _rope_tables(jnp.arange(seq)),
        jnp.zeros((1, batch, CONV_STATE, CONV_DIM), F32), 1, False, attend_prompt)

    cache_kr_t = jnp.swapaxes(cache_k_rope, 2, 3)

    def attend_sample(q, ckv, kpe, q_abs):
        o_lat = _paged_attention(
            page_table, q_abs.reshape(db, t_new, N_HEADS * KV_RANK),
            q.reshape(db, t_new, D_HEADS_PAD),
            ckv.reshape(db, t_new, KV_RANK), kpe.reshape(db, t_new, ROPE_DIM),
            cache_kv_latent, cache_kr_t)
        return o_lat.reshape(db * t_new, N_HEADS * KV_RANK)

    pos_s = jnp.tile(past_len + jnp.arange(t_new), TOKEN_TILE // t_new)
    y_s, ckv_s, kpe_s, cv_s = stream(x_sample, _rope_tables(pos_s), state_conv,
                                     CONV_BATCH, True, attend_sample)

    n_pg = seq // PAGE_SIZE
    return (y_p, y_s,
            ckv_p.reshape(1, batch, n_pg, PAGE_SIZE, KV_RANK),
            jnp.swapaxes(kpe_p.reshape(1, batch, n_pg, ROPE_DIM, PAGE_SIZE), 3, 4),
            cv_p[None],
            ckv_s.reshape(1, db, t_new, KV_RANK),
            kpe_s.reshape(1, db, t_new, ROPE_DIM),
            cv_s[None])
```

```python
import functools

import jax
import jax.numpy as jnp
from jax import lax
from jax.experimental import pallas as pl
from jax.experimental.pallas import tpu as pltpu

D_MODEL = 1024
N_HEADS = 8
QK_NOPE = 64
ROPE_DIM = 32
HALF_ROPE = ROPE_DIM // 2
V_DIM = 64
Q_RANK = 384
KV_RANK = 256
CONV_DIM = 512
CONV_WIDTH = 31
CONV_STATE = CONV_WIDTH - 1
D_ATTN = N_HEADS * V_DIM
D_FF = 2816
PAGE_SIZE = 128
ROPE_THETA = 10000.0
EPS = 1e-6
SM_SCALE = (QK_NOPE + ROPE_DIM) ** -0.5
LOG2_E = 1.4426950408889634
NEG_INF = -1e30

LANES = 128
SUBLANES = 8
HEAD_PAD = LANES
D_HEADS_PAD = N_HEADS * HEAD_PAD
KR_OFF = Q_RANK + KV_RANK
GA_OFF = KR_OFF + LANES
GB_OFF = GA_OFF + CONV_DIM
IN_COLS_PAD = GB_OFF + CONV_DIM
VMEM_LIMIT = 56 * 1024 * 1024

TOKEN_TILE = 512
ATTN_TILE = TOKEN_TILE
ONES_ROWS = 16
HEADS_AHEAD = 2
HEADS_PER_STEP = 8
CONV_TILE = 512
CONV_ROWS = 128
CONV_BATCH = 16
CHUNK_KEYS = 2048
SCORES_AHEAD = 3

BF16 = jnp.bfloat16
F32 = jnp.float32


def _dot(a, b):
    return jnp.dot(a, b, preferred_element_type=F32)


def _dot_nt(a, b):
    return lax.dot_general(a, b, (((1,), (1,)), ((), ())), preferred_element_type=F32)


def _rms(x, g, axis=-1):
    ms = jnp.mean(x * x, axis=axis, keepdims=True)
    return x * lax.rsqrt(ms + EPS) * g


def _rope_tile(blk, c, s1, s2):
    return (blk * c + pltpu.roll(blk, HALF_ROPE, 1) * s1
            + pltpu.roll(blk, LANES - HALF_ROPE, 1) * s2)


def _const_spec(shape):
    nd = len(shape)
    return pl.BlockSpec(shape, lambda *_: (0,) * nd, pipeline_mode=pl.Buffered(1))


def _params(sem):
    return pltpu.CompilerParams(dimension_semantics=sem, vmem_limit_bytes=VMEM_LIMIT)


def _half_ffn(x, g_ref, wg_ref, wu_ref, wd_ref):
    h = _rms(x, g_ref[...]).astype(BF16)
    a = _dot(h, wg_ref[...])
    a = a * jax.nn.sigmoid(a) * _dot(h, wu_ref[...])
    return x + 0.5 * _dot(a.astype(BF16), wd_ref[...])


def _ffn_kernel(x_ref, g_ref, wg_ref, wu_ref, wd_ref, o_ref):
    o_ref[...] = _half_ffn(x_ref[...], g_ref, wg_ref, wu_ref, wd_ref)


def _ffn_weight_specs():
    return [_const_spec((1, D_MODEL)), _const_spec((D_MODEL, D_FF)),
            _const_spec((D_MODEL, D_FF)), _const_spec((D_FF, D_MODEL))]


def _ffn(x, g, wg, wu, wd):
    n = x.shape[0]
    tm = TOKEN_TILE
    tok = pl.BlockSpec((tm, D_MODEL), lambda i: (i, 0))
    return pl.pallas_call(
        _ffn_kernel,
        out_shape=jax.ShapeDtypeStruct((n, D_MODEL), F32),
        grid=(n // tm,),
        in_specs=[tok] + _ffn_weight_specs(),
        out_specs=tok,
        compiler_params=_params(("parallel",)),
        name="ffn",
    )(x, g, wg, wu, wd)


def _merge_ffn_kernel(x_ref, a_ref, cv_ref, wuv_ref, ga_ref, wo_ref,
                      g_ref, wg_ref, wu_ref, wd_ref, gf_ref, o_ref, *, absorbed):
    if absorbed:
        attn = _dot(a_ref[...].astype(BF16), wuv_ref[...])
    else:
        attn = a_ref[...]
    mix = jnp.concatenate([_rms(attn, ga_ref[...]).astype(BF16), cv_ref[...].astype(BF16)], axis=1)
    x = x_ref[...] + _dot(mix, wo_ref[...])
    o_ref[...] = _rms(_half_ffn(x, g_ref, wg_ref, wu_ref, wd_ref), gf_ref[...])


def _merge_ffn(x, attn, conv, wuv, ga, wo, g, wg, wu, wd, gf, absorbed):
    n = x.shape[0]
    tm = TOKEN_TILE
    row = lambda w: pl.BlockSpec((tm, w), lambda i: (i, 0))
    return pl.pallas_call(
        functools.partial(_merge_ffn_kernel, absorbed=absorbed),
        out_shape=jax.ShapeDtypeStruct((n, D_MODEL), F32),
        grid=(n // tm,),
        in_specs=[row(D_MODEL), row(attn.shape[1]), row(CONV_DIM), _const_spec(wuv.shape),
                  _const_spec((1, D_ATTN)), _const_spec(wo.shape)]
                 + _ffn_weight_specs() + [_const_spec((1, D_MODEL))],
        out_specs=row(D_MODEL),
        compiler_params=_params(("parallel",)),
        name="merge_ffn_absorbed" if absorbed else "merge_ffn",
    )(x, attn, conv, wuv, ga, wo, g, wg, wu, wd, gf)


def _inproj_kernel(x_ref, gmix_ref, win_ref, gq_ref, wq_ref, gkv_ref, wa_ref, wb_ref,
                   c_ref, s1_ref, s2_ref, q_ref, ckv_ref, kpe_ref, glu_ref, *extra, absorbed):
    x = x_ref[...]
    h = _rms(x, gmix_ref[...]).astype(BF16)
    u = _dot(h, win_ref[...])
    c, s1, s2 = c_ref[...], s1_ref[...], s2_ref[...]

    qn = _rms(u[:, :Q_RANK], gq_ref[...]).astype(BF16)
    q_raw = _dot(qn, wq_ref[...])
    ckv = _rms(u[:, Q_RANK:KR_OFF], gkv_ref[...])
    ckv_ref[...] = ckv
    ckv_b = ckv.astype(BF16)
    kpe = _rope_tile(u[:, KR_OFF:GA_OFF], c, s1, s2)
    if absorbed:
        kpe_ref[...] = kpe[:, :ROPE_DIM]
    else:
        kpe_t = kpe.T[:ROPE_DIM]
        for pg in range(kpe_ref.shape[0]):
            kpe_ref[pg] = kpe_t[:, pg * PAGE_SIZE:(pg + 1) * PAGE_SIZE]
    glu_ref[...] = u[:, GA_OFF:GB_OFF] * jax.nn.sigmoid(u[:, GB_OFF:IN_COLS_PAD])

    if absorbed:
        (qabs_ref,) = extra
    else:
        k_ref, vt_ref = extra
        k_nope = _dot(ckv_b, wa_ref[...])
        vt_ref[0] = _dot_nt(wb_ref[...], ckv_b).astype(BF16)
    q_scale = SM_SCALE if absorbed else SM_SCALE * LOG2_E
    for hd in range(N_HEADS):
        sl = slice(hd * HEAD_PAD, (hd + 1) * HEAD_PAD)
        qh = (_rope_tile(q_raw[:, sl], c, s1, s2) * q_scale).astype(BF16)
        if absorbed:
            q_ref[:, sl] = qh.astype(F32)
            qabs_ref[:, hd * KV_RANK:(hd + 1) * KV_RANK] = (
                _dot(qh, wa_ref[hd]).astype(BF16).astype(F32))
        else:
            q_ref[hd] = qh
            k_ref[hd] = (k_nope[:, sl] + kpe).astype(BF16)


def _inproj(x, gmix, win, gq, wq, gkv, wa, wb, tables, absorbed):
    n = x.shape[0]
    tm = TOKEN_TILE
    row = lambda w: pl.BlockSpec((tm, w), lambda i: (i, 0))
    table_blocks = tables[0].shape[0] // tm
    table = pl.BlockSpec((tm, LANES), lambda i: (i % table_blocks, 0))
    heads_spec = pl.BlockSpec((N_HEADS, tm, HEAD_PAD), lambda i: (0, i, 0))
    heads_shape = jax.ShapeDtypeStruct((N_HEADS, n, HEAD_PAD), BF16)
    out_shape = [
        jax.ShapeDtypeStruct((n, D_HEADS_PAD), F32) if absorbed else heads_shape,
        jax.ShapeDtypeStruct((n, KV_RANK), F32),
        jax.ShapeDtypeStruct((n, ROPE_DIM) if absorbed else (n // PAGE_SIZE, ROPE_DIM, PAGE_SIZE),
                             F32),
        jax.ShapeDtypeStruct((n, CONV_DIM), F32),
    ]
    kpe_spec = row(ROPE_DIM) if absorbed else pl.BlockSpec(
        (tm // PAGE_SIZE, ROPE_DIM, PAGE_SIZE), lambda i: (i, 0, 0))
    out_specs = [row(D_HEADS_PAD) if absorbed else heads_spec, row(KV_RANK), kpe_spec,
                 row(CONV_DIM)]
    if absorbed:
        out_shape.append(jax.ShapeDtypeStruct((n, N_HEADS * KV_RANK), F32))
        out_specs.append(row(N_HEADS * KV_RANK))
    else:
        out_shape += [heads_shape, jax.ShapeDtypeStruct((n // tm, D_ATTN, tm), BF16)]
        out_specs += [heads_spec, pl.BlockSpec((1, D_ATTN, tm), lambda i: (i, 0, 0))]
    return pl.pallas_call(
        functools.partial(_inproj_kernel, absorbed=absorbed),
        out_shape=out_shape,
        grid=(n // tm,),
        in_specs=[row(D_MODEL), _const_spec((1, D_MODEL)), _const_spec(win.shape),
                  _const_spec((1, Q_RANK)), _const_spec(wq.shape), _const_spec((1, KV_RANK)),
                  _const_spec(wa.shape), _const_spec(wb.shape), table, table, table],
        out_specs=out_specs,
        compiler_params=_params(("parallel",)),
        name="inproj_absorbed" if absorbed else "inproj",
    )(x, gmix, win, gq, wq, gkv, wa, wb, *tables)


def _flash_kernel(q_ref, k_ref, vt_ref, o_ref, m_ref, acc_ref):
    t = ATTN_TILE
    half = t // 2
    qi = pl.program_id(2)
    heads = range(HEADS_PER_STEP)
    m_ref[...] = jnp.full(m_ref.shape, NEG_INF, F32)
    acc_ref[...] = jnp.zeros(acc_ref.shape, F32)

    def scores(j, h, n_keys, queries):
        k = k_ref[h, pl.ds(pl.multiple_of(j * t, t), n_keys), :]
        return _dot_nt(k, q_ref[h, queries, :])

    def update(j, h, s, n_keys, queries):
        vt = jnp.concatenate([vt_ref[j, h * V_DIM:(h + 1) * V_DIM, :n_keys],
                              jnp.ones((ONES_ROWS, n_keys), BF16)], axis=0)
        m = m_ref[h, :, queries]
        m_new = jnp.maximum(m, jnp.max(s, axis=0, keepdims=True))
        p = jnp.exp2(s - m_new).astype(BF16)
        acc_ref[h, :, queries] = jnp.exp2(m - m_new) * acc_ref[h, :, queries] + _dot(vt, p)
        m_ref[h, :, queries] = m_new

    def run(units, score_fn, update_fn):
        ahead = [score_fn(u) for u in units[:HEADS_AHEAD]]
        for i, u in enumerate(units):
            if i + HEADS_AHEAD < len(units):
                ahead.append(score_fn(units[i + HEADS_AHEAD]))
            update_fn(u, ahead[i])

    def full_tile(j, carry):
        everything = slice(0, t)
        run(list(heads), lambda h: scores(j, h, t, everything),
            lambda h, s: update(j, h, s, t, everything))
        return carry

    lax.fori_loop(0, qi, full_tile, 0)

    def diag_scores(u):
        h, part = u
        return scores(qi, h, (part + 1) * half, slice(part * half, (part + 1) * half))

    def diag_update(u, s):
        h, part = u
        n_keys = (part + 1) * half
        key_pos = lax.broadcasted_iota(jnp.int32, (n_keys, half), 0)
        query_pos = lax.broadcasted_iota(jnp.int32, (n_keys, half), 1) + part * half
        s = jnp.where(query_pos >= key_pos, s, NEG_INF)
        update(qi, h, s, n_keys, slice(part * half, (part + 1) * half))

    run([(h, part) for h in heads for part in range(2)], diag_scores, diag_update)
    for h in heads:
        acc = acc_ref[h]
        o_ref[:, h * V_DIM:(h + 1) * V_DIM] = (acc[:V_DIM] / acc[V_DIM:V_DIM + 1]).T


def _flash(q, k, vt, batch, seq):
    t = ATTN_TILE
    nq = seq // t
    rows = HEADS_PER_STEP * V_DIM
    return pl.pallas_call(
        _flash_kernel,
        out_shape=jax.ShapeDtypeStruct((batch * seq, D_ATTN), F32),
        grid=(batch, N_HEADS // HEADS_PER_STEP, nq),
        in_specs=[pl.BlockSpec((HEADS_PER_STEP, t, HEAD_PAD), lambda b, h, i: (h, b * nq + i, 0)),
                  pl.BlockSpec((HEADS_PER_STEP, seq, HEAD_PAD), lambda b, h, i: (h, b, 0)),
                  pl.BlockSpec((nq, rows, t), lambda b, h, i: (b, h, 0))],
        out_specs=pl.BlockSpec((t, rows), lambda b, h, i: (b * nq + i, h)),
        scratch_shapes=[pltpu.VMEM((HEADS_PER_STEP, 1, t), F32),
                        pltpu.VMEM((HEADS_PER_STEP, V_DIM + ONES_ROWS, t), F32)],
        compiler_params=_params(("parallel", "parallel", "arbitrary")),
        name="flash_prompt",
    )(q, k, vt)


def _paged_kernel(pt_ref, qa_ref, qp_ref, cn_ref, kn_ref, kv_hbm, kr_hbm, o_ref,
                  kvbuf, krbuf, sem, *, n_pages, n_new):
    b = pl.program_id(0)
    nb = pl.num_programs(0)
    slot = b % 2
    rows = n_new * N_HEADS

    def page_copies(bb, sl, j, off):
        page = pt_ref[bb * n_pages + j]
        return (pltpu.make_async_copy(kv_hbm.at[0, page],
                                      kvbuf.at[sl, pl.ds(off, PAGE_SIZE), :], sem.at[sl, 0]),
                pltpu.make_async_copy(kr_hbm.at[0, page], krbuf.at[sl, j], sem.at[sl, 1]))

    def wait_all(sl):
        pltpu.make_async_copy(kvbuf.at[sl], kvbuf.at[sl], sem.at[sl, 0]).wait()
        pltpu.make_async_copy(krbuf.at[sl], krbuf.at[sl], sem.at[sl, 1]).wait()

    @pl.when(b == 0)
    def _():
        def body(j, carry):
            for cp in page_copies(0, 0, j, pl.multiple_of(j * PAGE_SIZE, PAGE_SIZE)):
                cp.start()
            return carry
        lax.fori_loop(0, n_pages, body, 0)

    nxt = (b + 1) % nb
    for j in range(n_pages):
        for cp in page_copies(nxt, 1 - slot, j, j * PAGE_SIZE):
            cp.start()
    wait_all(slot)

    qa_all = qa_ref[0]
    qp_all = qp_ref[0]
    qa = jnp.concatenate([qa_all[:, h * KV_RANK:(h + 1) * KV_RANK] for h in range(N_HEADS)], axis=0)
    qp = jnp.concatenate([qp_all[:, h * HEAD_PAD:h * HEAD_PAD + ROPE_DIM]
                          for h in range(N_HEADS)], axis=0)

    cn = cn_ref[0]
    s = _dot_nt(qa, cn) + _dot_nt(qp, kn_ref[0])
    tq = lax.broadcasted_iota(jnp.int32, (rows, n_new), 0) % n_new
    tk = lax.broadcasted_iota(jnp.int32, (rows, n_new), 1)
    s = jnp.where(tq >= tk, s, NEG_INF)
    m = jnp.max(s, axis=-1, keepdims=True)
    p = jnp.exp(s - m)
    parts = [(m, jnp.sum(p, axis=-1, keepdims=True), _dot(p, cn))]

    n_chunks = n_pages * PAGE_SIZE // CHUNK_KEYS

    def chunk_kv(c):
        return kvbuf[slot, c * CHUNK_KEYS:(c + 1) * CHUNK_KEYS, :]

    def scores(c):
        pages = range(c * CHUNK_KEYS // PAGE_SIZE, (c + 1) * CHUNK_KEYS // PAGE_SIZE)
        krt = jnp.concatenate([krbuf[slot, p] for p in pages], axis=1)
        return _dot_nt(qa, chunk_kv(c)) + _dot(qp, krt)

    ahead = [scores(c) for c in range(min(SCORES_AHEAD, n_chunks))]
    for c in range(n_chunks):
        if c + SCORES_AHEAD < n_chunks:
            ahead.append(scores(c + SCORES_AHEAD))
        s = ahead[c]
        m = jnp.max(s, axis=-1, keepdims=True)
        p = jnp.exp(s - m)
        parts.append((m, jnp.sum(p, axis=-1, keepdims=True), _dot(p, chunk_kv(c))))

    m_all = functools.reduce(jnp.maximum, [m for m, _, _ in parts])
    l = jnp.zeros((rows, 1), F32)
    acc = jnp.zeros((rows, KV_RANK), F32)
    for m, l_c, acc_c in parts:
        w = jnp.exp(m - m_all)
        l = l + w * l_c
        acc = acc + w * acc_c
    o = acc / l
    for h in range(N_HEADS):
        o_ref[0, :, h * KV_RANK:(h + 1) * KV_RANK] = o[h * n_new:(h + 1) * n_new]

    @pl.when(b == nb - 1)
    def _():
        wait_all(1 - slot)


def _paged_attention(page_table, q_abs, q_rope, c_new, k_new, cache_kv, cache_kr_t):
    db, n_new = c_new.shape[0], c_new.shape[1]
    n_pages = page_table.shape[1]
    chunk_keys = n_pages * PAGE_SIZE
    assert chunk_keys % CHUNK_KEYS == 0 and n_new % SUBLANES == 0
    per_b = lambda w: pl.BlockSpec((1, n_new, w), lambda b, pt: (b, 0, 0))
    grid_spec = pltpu.PrefetchScalarGridSpec(
        num_scalar_prefetch=1,
        grid=(db,),
        in_specs=[per_b(N_HEADS * KV_RANK), per_b(D_HEADS_PAD), per_b(KV_RANK),
                  per_b(ROPE_DIM), pl.BlockSpec(memory_space=pl.ANY),
                  pl.BlockSpec(memory_space=pl.ANY)],
        out_specs=per_b(N_HEADS * KV_RANK),
        scratch_shapes=[pltpu.VMEM((2, chunk_keys, KV_RANK), F32),
                        pltpu.VMEM((2, n_pages, ROPE_DIM, PAGE_SIZE), F32),
                        pltpu.SemaphoreType.DMA((2, 2))],
    )
    return pl.pallas_call(
        functools.partial(_paged_kernel, n_pages=n_pages, n_new=n_new),
        out_shape=jax.ShapeDtypeStruct((db, n_new, N_HEADS * KV_RANK), F32),
        grid_spec=grid_spec,
        compiler_params=_params(("arbitrary",)),
        name="paged_attention",
    )(page_table.reshape(-1), q_abs, q_rope, c_new, k_new, cache_kv, cache_kr_t)


EXT_HEAD = 32


def _conv_kernel(g_ref, st_ref, w_ref, bdw_ref, gcn_ref, bcn_ref, gout_ref, o_ref, ext_ref, sh_ref,
                 *, tt):
    ti = pl.program_id(1)
    pad = EXT_HEAD - CONV_STATE
    rb = min(CONV_ROWS, tt)
    gb = g_ref.shape[0]
    lane_tiles = [slice(c * LANES, (c + 1) * LANES) for c in range(CONV_DIM // LANES)]

    @pl.when(ti == 0)
    def _():
        ext_ref[:, :, 0:pad, :] = jnp.zeros((gb, len(lane_tiles), pad, LANES), F32)
        for c, cs in enumerate(lane_tiles):
            ext_ref[:, c, pad:EXT_HEAD, :] = st_ref[:, :, cs]

    @pl.when(ti > 0)
    def _():
        ext_ref[:, :, 0:EXT_HEAD, :] = ext_ref[:, :, tt:tt + EXT_HEAD, :]

    for c, cs in enumerate(lane_tiles):
        ext_ref[:, c, EXT_HEAD:EXT_HEAD + tt, :] = g_ref[:, :, cs]

    span = tt + EXT_HEAD - SUBLANES
    for r in range(1, SUBLANES):
        sh_ref[r - 1, :, :, 0:span, :] = ext_ref[:, :, r:r + span, :]

    for c, cs in enumerate(lane_tiles):
        taps = [w_ref[k:k + 1, cs] for k in range(CONV_WIDTH)]
        bias = bdw_ref[:, cs]

        def block(i, carry, c=c, cs=cs, taps=taps, bias=bias):
            r0 = pl.multiple_of(i * rb, rb)
            acc = jnp.zeros((gb, rb, LANES), F32) + bias
            for o in range(pad, EXT_HEAD + 1):
                a, r = divmod(o, SUBLANES)
                rows = pl.ds(r0 + SUBLANES * a, rb)
                src = ext_ref[:, c, rows, :] if r == 0 else sh_ref[r - 1, :, c, rows, :]
                acc = acc + src * taps[o - pad]
            o_ref[:, pl.ds(r0, rb), cs] = acc
            return carry

        lax.fori_loop(0, tt // rb, block, 0)
    conv = o_ref[...]
    mu = jnp.mean(conv, axis=-1, keepdims=True)
    d = conv - mu
    var = jnp.mean(d * d, axis=-1, keepdims=True)
    y = d * lax.rsqrt(var + EPS) * gcn_ref[...] + bcn_ref[...]
    y = y * jax.nn.sigmoid(y)
    o_ref[...] = _rms(y, gout_ref[...])


def _conv(glu, state, w_dw, b_dw, g_cn, b_cn, g_out, tt, gb):
    b, t = glu.shape[0], glu.shape[1]
    vec = _const_spec((1, CONV_DIM))
    return pl.pallas_call(
        functools.partial(_conv_kernel, tt=tt),
        out_shape=jax.ShapeDtypeStruct((b, t, CONV_DIM), F32),
        grid=(b // gb, t // tt),
        in_specs=[pl.BlockSpec((gb, tt, CONV_DIM), lambda i, j: (i, j, 0)),
                  pl.BlockSpec((None, gb, CONV_STATE, CONV_DIM), lambda i, j: (0, i, 0, 0)),
                  _const_spec(w_dw.shape), vec, vec, vec, vec],
        out_specs=pl.BlockSpec((gb, tt, CONV_DIM), lambda i, j: (i, j, 0)),
        scratch_shapes=[pltpu.VMEM((gb, CONV_DIM // LANES, EXT_HEAD + tt, LANES), F32),
                        pltpu.VMEM((SUBLANES - 1, gb, CONV_DIM // LANES,
                                    EXT_HEAD + tt - SUBLANES, LANES), F32)],
        compiler_params=_params(("parallel", "arbitrary")),
        name="conv_module",
    )(glu, state, w_dw, b_dw, g_cn, b_cn, g_out)


def _pad_heads(w, lo):
    d = w.shape[-1]
    w = jnp.pad(w, [(0, 0)] * (w.ndim - 1) + [(lo, HEAD_PAD - lo - d)])
    return w.reshape(*w.shape[:-2], D_HEADS_PAD)


def _rope_tables(pos):
    inv_freq = ROPE_THETA ** (-jnp.arange(0, ROPE_DIM, 2, dtype=F32) / ROPE_DIM)
    ang = pos.astype(F32)[:, None] * inv_freq[None, :]
    cos, sin = jnp.cos(ang), jnp.sin(ang)
    rows = pos.shape[0]
    zeros = lambda w: jnp.zeros((rows, w), F32)
    c = jnp.concatenate([cos, cos, jnp.ones((rows, LANES - ROPE_DIM), F32)], axis=1)
    s1 = jnp.concatenate([zeros(HALF_ROPE), sin, zeros(LANES - ROPE_DIM)], axis=1)
    s2 = jnp.concatenate([-sin, zeros(LANES - HALF_ROPE)], axis=1)
    return c, s1, s2


def kernel(x_prompt, x_sample, cache_kv_latent, cache_k_rope, state_conv, page_table, g_ffn1, w1_gate, w1_up, w1_down, g_mix, w_in, g_q, w_q_b, g_kv, w_kv_b, w_dw, b_dw, g_cn, b_cn, g_out_attn, g_out_conv, w_out, g_ffn2, w2_gate, w2_up, w2_down, g_final):
    depth = g_ffn1.shape[0]
    assert depth == 1
    batch, seq, _ = x_prompt.shape
    db, t_new, _ = x_sample.shape
    n_pages = page_table.shape[1]
    past_len = n_pages * PAGE_SIZE
    assert seq % TOKEN_TILE == 0 and TOKEN_TILE % t_new == 0 and (db * t_new) % TOKEN_TILE == 0
    l = 0
    row = lambda v: v.reshape(1, -1)

    win = jnp.concatenate(
        [w_in[l][:, :KR_OFF + ROPE_DIM], jnp.zeros((D_MODEL, LANES - ROPE_DIM), F32),
         w_in[l][:, KR_OFF + ROPE_DIM:]], axis=1).astype(BF16)
    wq_heads = jnp.concatenate([w_q_b[l][..., QK_NOPE:], w_q_b[l][..., :QK_NOPE]], axis=-1)
    wq = _pad_heads(wq_heads, 0).astype(BF16)
    w_uk = w_kv_b[l][..., :QK_NOPE]
    w_uv = w_kv_b[l][..., QK_NOPE:]
    wk_exp = _pad_heads(w_uk, ROPE_DIM).astype(BF16)
    wv_t = w_uv.reshape(KV_RANK, D_ATTN).T.astype(BF16)
    wk_abs = jnp.pad(jnp.transpose(w_uk, (1, 2, 0)),
                     ((0, 0), (ROPE_DIM, HEAD_PAD - ROPE_DIM - QK_NOPE), (0, 0))).astype(BF16)
    eye = jnp.eye(N_HEADS, dtype=F32)
    wuv_bd = (jnp.transpose(w_uv, (1, 0, 2))[:, :, None, :] * eye[:, None, :, None])
    wuv_bd = wuv_bd.reshape(N_HEADS * KV_RANK, D_ATTN).astype(BF16)
    wo = w_out[l].astype(BF16)
    w1g, w1u, w1d = w1_gate[l].astype(BF16), w1_up[l].astype(BF16), w1_down[l].astype(BF16)
    w2g, w2u, w2d = w2_gate[l].astype(BF16), w2_up[l].astype(BF16), w2_down[l].astype(BF16)
    unused = jnp.zeros((SUBLANES, LANES), BF16)

    def stream(x, tables, conv_state, conv_batch, absorbed, attend):
        nb, nt = x.shape[0], x.shape[1]
        n = nb * nt
        x1 = _ffn(x.reshape(n, D_MODEL), row(g_ffn1[l]), w1g, w1u, w1d)
        wa, wb = (wk_abs, unused) if absorbed else (wk_exp, wv_t)
        q, ckv, kpe, glu, *extra = _inproj(x1, row(g_mix[l]), win, row(g_q[l]), wq, row(g_kv[l]),
                                           wa, wb, tables, absorbed)
        attn = attend(q, ckv, kpe, *extra)
        glu3 = glu.reshape(nb, nt, CONV_DIM)
        conv = _conv(glu3, conv_state, w_dw[l], row(b_dw[l]), row(g_cn[l]), row(b_cn[l]),
                     row(g_out_conv[l]), min(CONV_TILE, nt), conv_batch).reshape(n, CONV_DIM)
        y = _merge_ffn(x1, attn, conv, wuv_bd if absorbed else unused, row(g_out_attn[l]),
                       wo, row(g_ffn2[l]), w2g, w2u, w2d, row(g_final), absorbed)
        new_conv = jnp.concatenate([conv_state[0], glu3], axis=1)[:, -CONV_STATE:]
        return y.reshape(nb, nt, D_MODEL), ckv, kpe, new_conv

    def attend_prompt(q, ckv, kpe, k, vt):
        return _flash(q, k, vt, batch, seq)

    y_p, ckv_p, kpe_p, cv_p = stream(
        x_prompt, _rope_tables(jnp.arange(seq)),
        jnp.zeros((1, batch, CONV_STATE, CONV_DIM), F32), 1, False, attend_prompt)

    cache_kr_t = jnp.swapaxes(cache_k_rope, 2, 3)

    def attend_sample(q, ckv, kpe, q_abs):
        o_lat = _paged_attention(
            page_table, q_abs.reshape(db, t_new, N_HEADS * KV_RANK),
            q.reshape(db, t_new, D_HEADS_PAD),
            ckv.reshape(db, t_new, KV_RANK), kpe.reshape(db, t_new, ROPE_DIM),
            cache_kv_latent, cache_kr_t)
        return o_lat.reshape(db * t_new, N_HEADS * KV_RANK)

    pos_s = jnp.tile(past_len + jnp.arange(t_new), TOKEN_TILE // t_new)
    y_s, ckv_s, kpe_s, cv_s = stream(x_sample, _rope_tables(pos_s), state_conv,
                                     CONV_BATCH, True, attend_sample)

    n_pg = seq // PAGE_SIZE
    return (y_p, y_s,
            ckv_p.reshape(1, batch, n_pg, PAGE_SIZE, KV_RANK),
            jnp.swapaxes(kpe_p.reshape(1, batch, n_pg, ROPE_DIM, PAGE_SIZE), 3, 4),
            cv_p[None],
            ckv_s.reshape(1, db, t_new, KV_RANK),
            kpe_s.reshape(1, db, t_new, ROPE_DIM),
            cv_s[None])
```

```python
import functools

import jax
import jax.numpy as jnp
from jax import lax
from jax.experimental import pallas as pl
from jax.experimental.pallas import tpu as pltpu

D_MODEL = 1024
N_HEADS = 8
QK_NOPE = 64
ROPE_DIM = 32
HALF_ROPE = ROPE_DIM // 2
V_DIM = 64
Q_RANK = 384
KV_RANK = 256
CONV_DIM = 512
CONV_WIDTH = 31
CONV_STATE = CONV_WIDTH - 1
D_ATTN = N_HEADS * V_DIM
D_FF = 2816
PAGE_SIZE = 128
ROPE_THETA = 10000.0
EPS = 1e-6
SM_SCALE = (QK_NOPE + ROPE_DIM) ** -0.5
LOG2_E = 1.4426950408889634
NEG_INF = -1e30

LANES = 128
SUBLANES = 8
HEAD_PAD = LANES
D_HEADS_PAD = N_HEADS * HEAD_PAD
KR_OFF = Q_RANK + KV_RANK
GA_OFF = KR_OFF + LANES
GB_OFF = GA_OFF + CONV_DIM
IN_COLS_PAD = GB_OFF + CONV_DIM
VMEM_LIMIT = 56 * 1024 * 1024

TOKEN_TILE = 512
ATTN_TILE = TOKEN_TILE
ONES_ROWS = 16
HEADS_AHEAD = 2
HEADS_PER_STEP = 8
CONV_TILE = 512
CONV_ROWS = 128
CONV_BATCH = 16
ELEMS_PER_STEP = 2
CHUNK_KEYS = 2048
SCORES_AHEAD = 3

BF16 = jnp.bfloat16
F32 = jnp.float32


def _dot(a, b):
    return jnp.dot(a, b, preferred_element_type=F32)


def _dot_nt(a, b):
    return lax.dot_general(a, b, (((1,), (1,)), ((), ())), preferred_element_type=F32)


def _rms(x, g, axis=-1):
    ms = jnp.mean(x * x, axis=axis, keepdims=True)
    return x * lax.rsqrt(ms + EPS) * g


def _rope_tile(blk, c, s1, s2):
    return (blk * c + pltpu.roll(blk, HALF_ROPE, 1) * s1
            + pltpu.roll(blk, LANES - HALF_ROPE, 1) * s2)


def _const_spec(shape):
    nd = len(shape)
    return pl.BlockSpec(shape, lambda *_: (0,) * nd, pipeline_mode=pl.Buffered(1))


def _params(sem):
    return pltpu.CompilerParams(dimension_semantics=sem, vmem_limit_bytes=VMEM_LIMIT)


def _half_ffn(x, g_ref, wg_ref, wu_ref, wd_ref):
    h = _rms(x, g_ref[...]).astype(BF16)
    a = _dot(h, wg_ref[...])
    a = a * jax.nn.sigmoid(a) * _dot(h, wu_ref[...])
    return x + 0.5 * _dot(a.astype(BF16), wd_ref[...])


def _ffn_kernel(x_ref, g_ref, wg_ref, wu_ref, wd_ref, o_ref):
    o_ref[...] = _half_ffn(x_ref[...], g_ref, wg_ref, wu_ref, wd_ref)


def _ffn_weight_specs():
    return [_const_spec((1, D_MODEL)), _const_spec((D_MODEL, D_FF)),
            _const_spec((D_MODEL, D_FF)), _const_spec((D_FF, D_MODEL))]


def _ffn(x, g, wg, wu, wd):
    n = x.shape[0]
    tm = TOKEN_TILE
    tok = pl.BlockSpec((tm, D_MODEL), lambda i: (i, 0))
    return pl.pallas_call(
        _ffn_kernel,
        out_shape=jax.ShapeDtypeStruct((n, D_MODEL), F32),
        grid=(n // tm,),
        in_specs=[tok] + _ffn_weight_specs(),
        out_specs=tok,
        compiler_params=_params(("parallel",)),
        name="ffn",
    )(x, g, wg, wu, wd)


def _merge_ffn_kernel(x_ref, a_ref, cv_ref, wuv_ref, ga_ref, wo_ref,
                      g_ref, wg_ref, wu_ref, wd_ref, gf_ref, o_ref, *, absorbed):
    if absorbed:
        attn = _dot(a_ref[...].astype(BF16), wuv_ref[...])
    else:
        attn = a_ref[...]
    mix = jnp.concatenate([_rms(attn, ga_ref[...]).astype(BF16), cv_ref[...].astype(BF16)], axis=1)
    x = x_ref[...] + _dot(mix, wo_ref[...])
    o_ref[...] = _rms(_half_ffn(x, g_ref, wg_ref, wu_ref, wd_ref), gf_ref[...])


def _merge_ffn(x, attn, conv, wuv, ga, wo, g, wg, wu, wd, gf, absorbed):
    n = x.shape[0]
    tm = TOKEN_TILE
    row = lambda w: pl.BlockSpec((tm, w), lambda i: (i, 0))
    return pl.pallas_call(
        functools.partial(_merge_ffn_kernel, absorbed=absorbed),
        out_shape=jax.ShapeDtypeStruct((n, D_MODEL), F32),
        grid=(n // tm,),
        in_specs=[row(D_MODEL), row(attn.shape[1]), row(CONV_DIM), _const_spec(wuv.shape),
                  _const_spec((1, D_ATTN)), _const_spec(wo.shape)]
                 + _ffn_weight_specs() + [_const_spec((1, D_MODEL))],
        out_specs=row(D_MODEL),
        compiler_params=_params(("parallel",)),
        name="merge_ffn_absorbed" if absorbed else "merge_ffn",
    )(x, attn, conv, wuv, ga, wo, g, wg, wu, wd, gf)


def _inproj_kernel(x_ref, gmix_ref, win_ref, gq_ref, wq_ref, gkv_ref, wa_ref, wb_ref,
                   c_ref, s1_ref, s2_ref, q_ref, ckv_ref, kpe_ref, glu_ref, *extra, absorbed):
    x = x_ref[...]
    h = _rms(x, gmix_ref[...]).astype(BF16)
    u = _dot(h, win_ref[...])
    c, s1, s2 = c_ref[...], s1_ref[...], s2_ref[...]

    qn = _rms(u[:, :Q_RANK], gq_ref[...]).astype(BF16)
    q_raw = _dot(qn, wq_ref[...])
    ckv = _rms(u[:, Q_RANK:KR_OFF], gkv_ref[...])
    ckv_ref[...] = ckv
    ckv_b = ckv.astype(BF16)
    kpe = _rope_tile(u[:, KR_OFF:GA_OFF], c, s1, s2)
    if absorbed:
        kpe_ref[...] = kpe[:, :ROPE_DIM]
    else:
        kpe_t = kpe.T[:ROPE_DIM]
        for pg in range(kpe_ref.shape[0]):
            kpe_ref[pg] = kpe_t[:, pg * PAGE_SIZE:(pg + 1) * PAGE_SIZE]
    glu_ref[...] = u[:, GA_OFF:GB_OFF] * jax.nn.sigmoid(u[:, GB_OFF:IN_COLS_PAD])

    if absorbed:
        (qabs_ref,) = extra
    else:
        k_ref, vt_ref = extra
        k_nope = _dot(ckv_b, wa_ref[...])
        vt_ref[0] = _dot_nt(wb_ref[...], ckv_b).astype(BF16)
    q_scale = SM_SCALE if absorbed else SM_SCALE * LOG2_E
    for hd in range(N_HEADS):
        sl = slice(hd * HEAD_PAD, (hd + 1) * HEAD_PAD)
        qh = (_rope_tile(q_raw[:, sl], c, s1, s2) * q_scale).astype(BF16)
        if absorbed:
            q_ref[:, sl] = qh.astype(F32)
            qabs_ref[:, hd * KV_RANK:(hd + 1) * KV_RANK] = (
                _dot(qh, wa_ref[hd]).astype(BF16).astype(F32))
        else:
            q_ref[hd] = qh
            k_ref[hd] = (k_nope[:, sl] + kpe).astype(BF16)


def _inproj(x, gmix, win, gq, wq, gkv, wa, wb, tables, absorbed):
    n = x.shape[0]
    tm = TOKEN_TILE
    row = lambda w: pl.BlockSpec((tm, w), lambda i: (i, 0))
    table_blocks = tables[0].shape[0] // tm
    table = pl.BlockSpec((tm, LANES), lambda i: (i % table_blocks, 0))
    heads_spec = pl.BlockSpec((N_HEADS, tm, HEAD_PAD), lambda i: (0, i, 0))
    heads_shape = jax.ShapeDtypeStruct((N_HEADS, n, HEAD_PAD), BF16)
    out_shape = [
        jax.ShapeDtypeStruct((n, D_HEADS_PAD), F32) if absorbed else heads_shape,
        jax.ShapeDtypeStruct((n, KV_RANK), F32),
        jax.ShapeDtypeStruct((n, ROPE_DIM) if absorbed else (n // PAGE_SIZE, ROPE_DIM, PAGE_SIZE),
                             F32),
        jax.ShapeDtypeStruct((n, CONV_DIM), F32),
    ]
    kpe_spec = row(ROPE_DIM) if absorbed else pl.BlockSpec(
        (tm // PAGE_SIZE, ROPE_DIM, PAGE_SIZE), lambda i: (i, 0, 0))
    out_specs = [row(D_HEADS_PAD) if absorbed else heads_spec, row(KV_RANK), kpe_spec,
                 row(CONV_DIM)]
    if absorbed:
        out_shape.append(jax.ShapeDtypeStruct((n, N_HEADS * KV_RANK), F32))
        out_specs.append(row(N_HEADS * KV_RANK))
    else:
        out_shape += [heads_shape, jax.ShapeDtypeStruct((n // tm, D_ATTN, tm), BF16)]
        out_specs += [heads_spec, pl.BlockSpec((1, D_ATTN, tm), lambda i: (i, 0, 0))]
    return pl.pallas_call(
        functools.partial(_inproj_kernel, absorbed=absorbed),
        out_shape=out_shape,
        grid=(n // tm,),
        in_specs=[row(D_MODEL), _const_spec((1, D_MODEL)), _const_spec(win.shape),
                  _const_spec((1, Q_RANK)), _const_spec(wq.shape), _const_spec((1, KV_RANK)),
                  _const_spec(wa.shape), _const_spec(wb.shape), table, table, table],
        out_specs=out_specs,
        compiler_params=_params(("parallel",)),
        name="inproj_absorbed" if absorbed else "inproj",
    )(x, gmix, win, gq, wq, gkv, wa, wb, *tables)


def _flash_kernel(q_ref, k_ref, vt_ref, o_ref, m_ref, acc_ref):
    t = ATTN_TILE
    half = t // 2
    qi = pl.program_id(2)
    heads = range(HEADS_PER_STEP)
    m_ref[...] = jnp.full(m_ref.shape, NEG_INF, F32)
    acc_ref[...] = jnp.zeros(acc_ref.shape, F32)

    def scores(j, h, n_keys, queries):
        k = k_ref[h, pl.ds(pl.multiple_of(j * t, t), n_keys), :]
        return _dot_nt(k, q_ref[h, queries, :])

    def update(j, h, s, n_keys, queries):
        vt = jnp.concatenate([vt_ref[j, h * V_DIM:(h + 1) * V_DIM, :n_keys],
                              jnp.ones((ONES_ROWS, n_keys), BF16)], axis=0)
        m = m_ref[h, :, queries]
        m_new = jnp.maximum(m, jnp.max(s, axis=0, keepdims=True))
        p = jnp.exp2(s - m_new).astype(BF16)
        acc_ref[h, :, queries] = jnp.exp2(m - m_new) * acc_ref[h, :, queries] + _dot(vt, p)
        m_ref[h, :, queries] = m_new

    def run(units, score_fn, update_fn):
        ahead = [score_fn(u) for u in units[:HEADS_AHEAD]]
        for i, u in enumerate(units):
            if i + HEADS_AHEAD < len(units):
                ahead.append(score_fn(units[i + HEADS_AHEAD]))
            update_fn(u, ahead[i])

    def full_tile(j, carry):
        everything = slice(0, t)
        run(list(heads), lambda h: scores(j, h, t, everything),
            lambda h, s: update(j, h, s, t, everything))
        return carry

    lax.fori_loop(0, qi, full_tile, 0)

    def diag_scores(u):
        h, part = u
        return scores(qi, h, (part + 1) * half, slice(part * half, (part + 1) * half))

    def diag_update(u, s):
        h, part = u
        n_keys = (part + 1) * half
        key_pos = lax.broadcasted_iota(jnp.int32, (n_keys, half), 0)
        query_pos = lax.broadcasted_iota(jnp.int32, (n_keys, half), 1) + part * half
        s = jnp.where(query_pos >= key_pos, s, NEG_INF)
        update(qi, h, s, n_keys, slice(part * half, (part + 1) * half))

    run([(h, part) for h in heads for part in range(2)], diag_scores, diag_update)
    for h in heads:
        acc = acc_ref[h]
        o_ref[:, h * V_DIM:(h + 1) * V_DIM] = (acc[:V_DIM] / acc[V_DIM:V_DIM + 1]).T


def _flash(q, k, vt, batch, seq):
    t = ATTN_TILE
    nq = seq // t
    rows = HEADS_PER_STEP * V_DIM
    return pl.pallas_call(
        _flash_kernel,
        out_shape=jax.ShapeDtypeStruct((batch * seq, D_ATTN), F32),
        grid=(batch, N_HEADS // HEADS_PER_STEP, nq),
        in_specs=[pl.BlockSpec((HEADS_PER_STEP, t, HEAD_PAD), lambda b, h, i: (h, b * nq + i, 0)),
                  pl.BlockSpec((HEADS_PER_STEP, seq, HEAD_PAD), lambda b, h, i: (h, b, 0)),
                  pl.BlockSpec((nq, rows, t), lambda b, h, i: (b, h, 0))],
        out_specs=pl.BlockSpec((t, rows), lambda b, h, i: (b * nq + i, h)),
        scratch_shapes=[pltpu.VMEM((HEADS_PER_STEP, 1, t), F32),
                        pltpu.VMEM((HEADS_PER_STEP, V_DIM + ONES_ROWS, t), F32)],
        compiler_params=_params(("parallel", "parallel", "arbitrary")),
        name="flash_prompt",
    )(q, k, vt)


def _paged_kernel(pt_ref, qa_ref, qp_ref, cn_ref, kn_ref, kv_hbm, kr_hbm, o_ref,
                  kv_a, kr_a, kv_b, kr_b, sem, *, n_pages, n_new):
    step = pl.program_id(0)
    n_elems = ELEMS_PER_STEP * pl.num_programs(0)
    slots = ((kv_a, kr_a, 0), (kv_b, kr_b, 1))

    def page_copies(elem, slot, j, off):
        kvbuf, krbuf, si = slot
        page = pt_ref[elem * n_pages + j]
        return (pltpu.make_async_copy(kv_hbm.at[0, page],
                                      kvbuf.at[pl.ds(off, PAGE_SIZE), :], sem.at[si, 0]),
                pltpu.make_async_copy(kr_hbm.at[0, page], krbuf.at[j], sem.at[si, 1]))

    def start_all(elem, slot):
        for j in range(n_pages):
            for cp in page_copies(elem, slot, j, j * PAGE_SIZE):
                cp.start()

    def wait_all(slot):
        kvbuf, krbuf, si = slot
        pltpu.make_async_copy(kvbuf, kvbuf, sem.at[si, 0]).wait()
        pltpu.make_async_copy(krbuf, krbuf, sem.at[si, 1]).wait()

    @pl.when(step == 0)
    def _():
        def body(j, carry):
            for cp in page_copies(0, slots[0], j, pl.multiple_of(j * PAGE_SIZE, PAGE_SIZE)):
                cp.start()
            return carry
        lax.fori_loop(0, n_pages, body, 0)

    first = step * ELEMS_PER_STEP
    wait_all(slots[0])
    start_all(first + 1, slots[1])
    _paged_element(0, slots[0], qa_ref, qp_ref, cn_ref, kn_ref, o_ref, n_pages, n_new)
    start_all((first + 2) % n_elems, slots[0])
    wait_all(slots[1])
    _paged_element(1, slots[1], qa_ref, qp_ref, cn_ref, kn_ref, o_ref, n_pages, n_new)

    @pl.when(step == pl.num_programs(0) - 1)
    def _():
        wait_all(slots[0])


def _paged_element(e, slot, qa_ref, qp_ref, cn_ref, kn_ref, o_ref, n_pages, n_new):
    kvbuf, krbuf, _ = slot
    rows = n_new * N_HEADS
    qa_all = qa_ref[e]
    qp_all = qp_ref[e]
    qa = jnp.concatenate([qa_all[:, h * KV_RANK:(h + 1) * KV_RANK] for h in range(N_HEADS)], axis=0)
    qp = jnp.concatenate([qp_all[:, h * HEAD_PAD:h * HEAD_PAD + ROPE_DIM]
                          for h in range(N_HEADS)], axis=0)

    cn = cn_ref[e]
    s = _dot_nt(qa, cn) + _dot_nt(qp, kn_ref[e])
    tq = lax.broadcasted_iota(jnp.int32, (rows, n_new), 0) % n_new
    tk = lax.broadcasted_iota(jnp.int32, (rows, n_new), 1)
    s = jnp.where(tq >= tk, s, NEG_INF)
    m = jnp.max(s, axis=-1, keepdims=True)
    p = jnp.exp(s - m)
    parts = [(m, jnp.sum(p, axis=-1, keepdims=True), _dot(p, cn))]

    n_chunks = n_pages * PAGE_SIZE // CHUNK_KEYS

    def chunk_kv(c):
        return kvbuf[c * CHUNK_KEYS:(c + 1) * CHUNK_KEYS, :]

    def scores(c):
        pages = range(c * CHUNK_KEYS // PAGE_SIZE, (c + 1) * CHUNK_KEYS // PAGE_SIZE)
        krt = jnp.concatenate([krbuf[p] for p in pages], axis=1)
        return _dot_nt(qa, chunk_kv(c)) + _dot(qp, krt)

    ahead = [scores(c) for c in range(min(SCORES_AHEAD, n_chunks))]
    for c in range(n_chunks):
        if c + SCORES_AHEAD < n_chunks:
            ahead.append(scores(c + SCORES_AHEAD))
        s = ahead[c]
        m = jnp.max(s, axis=-1, keepdims=True)
        p = jnp.exp(s - m)
        parts.append((m, jnp.sum(p, axis=-1, keepdims=True), _dot(p, chunk_kv(c))))

    m_all = functools.reduce(jnp.maximum, [m for m, _, _ in parts])
    l = jnp.zeros((rows, 1), F32)
    acc = jnp.zeros((rows, KV_RANK), F32)
    for m, l_c, acc_c in parts:
        w = jnp.exp(m - m_all)
        l = l + w * l_c
        acc = acc + w * acc_c
    o = acc / l
    for h in range(N_HEADS):
        o_ref[e, :, h * KV_RANK:(h + 1) * KV_RANK] = o[h * n_new:(h + 1) * n_new]


def _paged_attention(page_table, q_abs, q_rope, c_new, k_new, cache_kv, cache_kr_t):
    db, n_new = c_new.shape[0], c_new.shape[1]
    n_pages = page_table.shape[1]
    slot_keys = n_pages * PAGE_SIZE
    assert slot_keys % CHUNK_KEYS == 0 and n_new % SUBLANES == 0 and db % ELEMS_PER_STEP == 0
    per_b = lambda w: pl.BlockSpec((ELEMS_PER_STEP, n_new, w), lambda b, pt: (b, 0, 0))
    slot = [pltpu.VMEM((slot_keys, KV_RANK), F32), pltpu.VMEM((n_pages, ROPE_DIM, PAGE_SIZE), F32)]
    grid_spec = pltpu.PrefetchScalarGridSpec(
        num_scalar_prefetch=1,
        grid=(db // ELEMS_PER_STEP,),
        in_specs=[per_b(N_HEADS * KV_RANK), per_b(D_HEADS_PAD), per_b(KV_RANK),
                  per_b(ROPE_DIM), pl.BlockSpec(memory_space=pl.ANY),
                  pl.BlockSpec(memory_space=pl.ANY)],
        out_specs=per_b(N_HEADS * KV_RANK),
        scratch_shapes=slot + slot + [pltpu.SemaphoreType.DMA((ELEMS_PER_STEP, 2))],
    )
    return pl.pallas_call(
        functools.partial(_paged_kernel, n_pages=n_pages, n_new=n_new),
        out_shape=jax.ShapeDtypeStruct((db, n_new, N_HEADS * KV_RANK), F32),
        grid_spec=grid_spec,
        compiler_params=_params(("arbitrary",)),
        name="paged_attention",
    )(page_table.reshape(-1), q_abs, q_rope, c_new, k_new, cache_kv, cache_kr_t)


EXT_HEAD = 32


def _conv_kernel(g_ref, st_ref, w_ref, bdw_ref, gcn_ref, bcn_ref, gout_ref, o_ref, ext_ref, sh_ref,
                 *, tt):
    ti = pl.program_id(1)
    pad = EXT_HEAD - CONV_STATE
    rb = min(CONV_ROWS, tt)
    gb = g_ref.shape[0]
    lane_tiles = [slice(c * LANES, (c + 1) * LANES) for c in range(CONV_DIM // LANES)]

    @pl.when(ti == 0)
    def _():
        ext_ref[:, :, 0:pad, :] = jnp.zeros((gb, len(lane_tiles), pad, LANES), F32)
        for c, cs in enumerate(lane_tiles):
            ext_ref[:, c, pad:EXT_HEAD, :] = st_ref[:, :, cs]

    @pl.when(ti > 0)
    def _():
        ext_ref[:, :, 0:EXT_HEAD, :] = ext_ref[:, :, tt:tt + EXT_HEAD, :]

    for c, cs in enumerate(lane_tiles):
        ext_ref[:, c, EXT_HEAD:EXT_HEAD + tt, :] = g_ref[:, :, cs]

    span = tt + EXT_HEAD - SUBLANES
    for r in range(1, SUBLANES):
        sh_ref[r - 1, :, :, 0:span, :] = ext_ref[:, :, r:r + span, :]

    for c, cs in enumerate(lane_tiles):
        taps = [w_ref[k:k + 1, cs] for k in range(CONV_WIDTH)]
        bias = bdw_ref[:, cs]

        def block(i, carry, c=c, cs=cs, taps=taps, bias=bias):
            r0 = pl.multiple_of(i * rb, rb)
            acc = jnp.zeros((gb, rb, LANES), F32) + bias
            for o in range(pad, EXT_HEAD + 1):
                a, r = divmod(o, SUBLANES)
                rows = pl.ds(r0 + SUBLANES * a, rb)
                src = ext_ref[:, c, rows, :] if r == 0 else sh_ref[r - 1, :, c, rows, :]
                acc = acc + src * taps[o - pad]
            o_ref[:, pl.ds(r0, rb), cs] = acc
            return carry

        lax.fori_loop(0, tt // rb, block, 0)
    conv = o_ref[...]
    mu = jnp.mean(conv, axis=-1, keepdims=True)
    d = conv - mu
    var = jnp.mean(d * d, axis=-1, keepdims=True)
    y = d * lax.rsqrt(var + EPS) * gcn_ref[...] + bcn_ref[...]
    y = y * jax.nn.sigmoid(y)
    o_ref[...] = _rms(y, gout_ref[...])


def _conv(glu, state, w_dw, b_dw, g_cn, b_cn, g_out, tt, gb):
    b, t = glu.shape[0], glu.shape[1]
    vec = _const_spec((1, CONV_DIM))
    return pl.pallas_call(
        functools.partial(_conv_kernel, tt=tt),
        out_shape=jax.ShapeDtypeStruct((b, t, CONV_DIM), F32),
        grid=(b // gb, t // tt),
        in_specs=[pl.BlockSpec((gb, tt, CONV_DIM), lambda i, j: (i, j, 0)),
                  pl.BlockSpec((None, gb, CONV_STATE, CONV_DIM), lambda i, j: (0, i, 0, 0)),
                  _const_spec(w_dw.shape), vec, vec, vec, vec],
        out_specs=pl.BlockSpec((gb, tt, CONV_DIM), lambda i, j: (i, j, 0)),
        scratch_shapes=[pltpu.VMEM((gb, CONV_DIM // LANES, EXT_HEAD + tt, LANES), F32),
                        pltpu.VMEM((SUBLANES - 1, gb, CONV_DIM // LANES,
                                    EXT_HEAD + tt - SUBLANES, LANES), F32)],
        compiler_params=_params(("parallel", "arbitrary")),
        name="conv_module",
    )(glu, state, w_dw, b_dw, g_cn, b_cn, g_out)


def _pad_heads(w, lo):
    d = w.shape[-1]
    w = jnp.pad(w, [(0, 0)] * (w.ndim - 1) + [(lo, HEAD_PAD - lo - d)])
    return w.reshape(*w.shape[:-2], D_HEADS_PAD)


def _rope_tables(pos):
    inv_freq = ROPE_THETA ** (-jnp.arange(0, ROPE_DIM, 2, dtype=F32) / ROPE_DIM)
    ang = pos.astype(F32)[:, None] * inv_freq[None, :]
    cos, sin = jnp.cos(ang), jnp.sin(ang)
    rows = pos.shape[0]
    zeros = lambda w: jnp.zeros((rows, w), F32)
    c = jnp.concatenate([cos, cos, jnp.ones((rows, LANES - ROPE_DIM), F32)], axis=1)
    s1 = jnp.concatenate([zeros(HALF_ROPE), sin, zeros(LANES - ROPE_DIM)], axis=1)
    s2 = jnp.concatenate([-sin, zeros(LANES - HALF_ROPE)], axis=1)
    return c, s1, s2


def kernel(x_prompt, x_sample, cache_kv_latent, cache_k_rope, state_conv, page_table, g_ffn1, w1_gate, w1_up, w1_down, g_mix, w_in, g_q, w_q_b, g_kv, w_kv_b, w_dw, b_dw, g_cn, b_cn, g_out_attn, g_out_conv, w_out, g_ffn2, w2_gate, w2_up, w2_down, g_final):
    depth = g_ffn1.shape[0]
    assert depth == 1
    batch, seq, _ = x_prompt.shape
    db, t_new, _ = x_sample.shape
    n_pages = page_table.shape[1]
    past_len = n_pages * PAGE_SIZE
    assert seq % TOKEN_TILE == 0 and TOKEN_TILE % t_new == 0 and (db * t_new) % TOKEN_TILE == 0
    l = 0
    row = lambda v: v.reshape(1, -1)

    win = jnp.concatenate(
        [w_in[l][:, :KR_OFF + ROPE_DIM], jnp.zeros((D_MODEL, LANES - ROPE_DIM), F32),
         w_in[l][:, KR_OFF + ROPE_DIM:]], axis=1).astype(BF16)
    wq_heads = jnp.concatenate([w_q_b[l][..., QK_NOPE:], w_q_b[l][..., :QK_NOPE]], axis=-1)
    wq = _pad_heads(wq_heads, 0).astype(BF16)
    w_uk = w_kv_b[l][..., :QK_NOPE]
    w_uv = w_kv_b[l][..., QK_NOPE:]
    wk_exp = _pad_heads(w_uk, ROPE_DIM).astype(BF16)
    wv_t = w_uv.reshape(KV_RANK, D_ATTN).T.astype(BF16)
    wk_abs = jnp.pad(jnp.transpose(w_uk, (1, 2, 0)),
                     ((0, 0), (ROPE_DIM, HEAD_PAD - ROPE_DIM - QK_NOPE), (0, 0))).astype(BF16)
    eye = jnp.eye(N_HEADS, dtype=F32)
    wuv_bd = (jnp.transpose(w_uv, (1, 0, 2))[:, :, None, :] * eye[:, None, :, None])
    wuv_bd = wuv_bd.reshape(N_HEADS * KV_RANK, D_ATTN).astype(BF16)
    wo = w_out[l].astype(BF16)
    w1g, w1u, w1d = w1_gate[l].astype(BF16), w1_up[l].astype(BF16), w1_down[l].astype(BF16)
    w2g, w2u, w2d = w2_gate[l].astype(BF16), w2_up[l].astype(BF16), w2_down[l].astype(BF16)
    unused = jnp.zeros((SUBLANES, LANES), BF16)

    def stream(x, tables, conv_state, conv_batch, absorbed, attend):
        nb, nt = x.shape[0], x.shape[1]
        n = nb * nt
        x1 = _ffn(x.reshape(n, D_MODEL), row(g_ffn1[l]), w1g, w1u, w1d)
        wa, wb = (wk_abs, unused) if absorbed else (wk_exp, wv_t)
        q, ckv, kpe, glu, *extra = _inproj(x1, row(g_mix[l]), win, row(g_q[l]), wq, row(g_kv[l]),
                                           wa, wb, tables, absorbed)
        attn = attend(q, ckv, kpe, *extra)
        glu3 = glu.reshape(nb, nt, CONV_DIM)
        conv = _conv(glu3, conv_state, w_dw[l], row(b_dw[l]), row(g_cn[l]), row(b_cn[l]),
                     row(g_out_conv[l]), min(CONV_TILE, nt), conv_batch).reshape(n, CONV_DIM)
        y = _merge_ffn(x1, attn, conv, wuv_bd if absorbed else unused, row(g_out_attn[l]),
                       wo, row(g_ffn2[l]), w2g, w2u, w2d, row(g_final), absorbed)
        new_conv = jnp.concatenate([conv_state[0], glu3], axis=1)[:, -CONV_STATE:]
        return y.reshape(nb, nt, D_MODEL), ckv, kpe, new_conv

    def attend_prompt(q, ckv, kpe, k, vt):
        return _flash(q, k, vt, batch, seq)

    y_p, ckv_p, kpe_p, cv_p = stream(
        x_prompt, _rope_tables(jnp.arange(seq)),
        jnp.zeros((1, batch, CONV_STATE, CONV_DIM), F32), 1, False, attend_prompt)

    cache_kr_t = jnp.swapaxes(cache_k_rope, 2, 3)

    def attend_sample(q, ckv, kpe, q_abs):
        o_lat = _paged_attention(
            page_table, q_abs.reshape(db, t_new, N_HEADS * KV_RANK),
            q.reshape(db, t_new, D_HEADS_PAD),
            ckv.reshape(db, t_new, KV_RANK), kpe.reshape(db, t_new, ROPE_DIM),
            cache_kv_latent, cache_kr_t)
        return o_lat.reshape(db * t_new, N_HEADS * KV_RANK)

    pos_s = jnp.tile(past_len + jnp.arange(t_new), TOKEN_TILE // t_new)
    y_s, ckv_s, kpe_s, cv_s = stream(x_sample, _rope_tables(pos_s), state_conv,
                                     CONV_BATCH, True, attend_sample)

    n_pg = seq // PAGE_SIZE
    return (y_p, y_s,
            ckv_p.reshape(1, batch, n_pg, PAGE_SIZE, KV_RANK),
            jnp.swapaxes(kpe_p.reshape(1, batch, n_pg, ROPE_DIM, PAGE_SIZE), 3, 4),
            cv_p[None],
            ckv_s.reshape(1, db, t_new, KV_RANK),
            kpe_s.reshape(1, db, t_new, ROPE_DIM),
            cv_s[None])
```

```python
import functools

import jax
import jax.numpy as jnp
from jax import lax
from jax.experimental import pallas as pl
from jax.experimental.pallas import tpu as pltpu

D_MODEL = 1024
N_HEADS = 8
QK_NOPE = 64
ROPE_DIM = 32
HALF_ROPE = ROPE_DIM // 2
V_DIM = 64
Q_RANK = 384
KV_RANK = 256
CONV_DIM = 512
CONV_WIDTH = 31
CONV_STATE = CONV_WIDTH - 1
D_ATTN = N_HEADS * V_DIM
D_FF = 2816
PAGE_SIZE = 128
ROPE_THETA = 10000.0
EPS = 1e-6
SM_SCALE = (QK_NOPE + ROPE_DIM) ** -0.5
LOG2_E = 1.4426950408889634
NEG_INF = -1e30

LANES = 128
SUBLANES = 8
HEAD_PAD = LANES
D_HEADS_PAD = N_HEADS * HEAD_PAD
KR_OFF = Q_RANK + KV_RANK
GA_OFF = KR_OFF + LANES
GB_OFF = GA_OFF + CONV_DIM
IN_COLS_PAD = GB_OFF + CONV_DIM
VMEM_LIMIT = 56 * 1024 * 1024

TOKEN_TILE = 512
ATTN_TILE = TOKEN_TILE
ONES_ROWS = 16
HEADS_AHEAD = 2
HEADS_PER_STEP = 8
CONV_TILE = 512
CONV_ROWS = 128
CONV_BATCH = 16
CHUNK_KEYS = 2048
SCORES_AHEAD = 3

BF16 = jnp.bfloat16
F32 = jnp.float32


def _dot(a, b):
    return jnp.dot(a, b, preferred_element_type=F32)


def _dot_nt(a, b):
    return lax.dot_general(a, b, (((1,), (1,)), ((), ())), preferred_element_type=F32)


def _rms(x, g, axis=-1):
    ms = jnp.mean(x * x, axis=axis, keepdims=True)
    return x * lax.rsqrt(ms + EPS) * g


def _rope_tile(blk, c, s1, s2):
    return (blk * c + pltpu.roll(blk, HALF_ROPE, 1) * s1
            + pltpu.roll(blk, LANES - HALF_ROPE, 1) * s2)


def _const_spec(shape):
    nd = len(shape)
    return pl.BlockSpec(shape, lambda *_: (0,) * nd, pipeline_mode=pl.Buffered(1))


def _params(sem):
    return pltpu.CompilerParams(dimension_semantics=sem, vmem_limit_bytes=VMEM_LIMIT)


def _half_ffn(x, g_ref, wg_ref, wu_ref, wd_ref):
    h = _rms(x, g_ref[...]).astype(BF16)
    a = _dot(h, wg_ref[...])
    a = a * jax.nn.sigmoid(a) * _dot(h, wu_ref[...])
    return x + 0.5 * _dot(a.astype(BF16), wd_ref[...])


def _ffn_kernel(x_ref, g_ref, wg_ref, wu_ref, wd_ref, o_ref):
    o_ref[...] = _half_ffn(x_ref[...], g_ref, wg_ref, wu_ref, wd_ref)


def _ffn_weight_specs():
    return [_const_spec((1, D_MODEL)), _const_spec((D_MODEL, D_FF)),
            _const_spec((D_MODEL, D_FF)), _const_spec((D_FF, D_MODEL))]


def _ffn(x, g, wg, wu, wd):
    n = x.shape[0]
    tm = TOKEN_TILE
    tok = pl.BlockSpec((tm, D_MODEL), lambda i: (i, 0))
    return pl.pallas_call(
        _ffn_kernel,
        out_shape=jax.ShapeDtypeStruct((n, D_MODEL), F32),
        grid=(n // tm,),
        in_specs=[tok] + _ffn_weight_specs(),
        out_specs=tok,
        compiler_params=_params(("parallel",)),
        name="ffn",
    )(x, g, wg, wu, wd)


def _merge_ffn_kernel(x_ref, a_ref, cv_ref, wuv_ref, ga_ref, wo_ref,
                      g_ref, wg_ref, wu_ref, wd_ref, gf_ref, o_ref, *, absorbed):
    if absorbed:
        attn = _dot(a_ref[...].astype(BF16), wuv_ref[...])
    else:
        attn = a_ref[...]
    mix = jnp.concatenate([_rms(attn, ga_ref[...]).astype(BF16), cv_ref[...].astype(BF16)], axis=1)
    x = x_ref[...] + _dot(mix, wo_ref[...])
    o_ref[...] = _rms(_half_ffn(x, g_ref, wg_ref, wu_ref, wd_ref), gf_ref[...])


def _merge_ffn(x, attn, conv, wuv, ga, wo, g, wg, wu, wd, gf, absorbed):
    n = x.shape[0]
    tm = TOKEN_TILE
    row = lambda w: pl.BlockSpec((tm, w), lambda i: (i, 0))
    return pl.pallas_call(
        functools.partial(_merge_ffn_kernel, absorbed=absorbed),
        out_shape=jax.ShapeDtypeStruct((n, D_MODEL), F32),
        grid=(n // tm,),
        in_specs=[row(D_MODEL), row(attn.shape[1]), row(CONV_DIM), _const_spec(wuv.shape),
                  _const_spec((1, D_ATTN)), _const_spec(wo.shape)]
                 + _ffn_weight_specs() + [_const_spec((1, D_MODEL))],
        out_specs=row(D_MODEL),
        compiler_params=_params(("parallel",)),
        name="merge_ffn_absorbed" if absorbed else "merge_ffn",
    )(x, attn, conv, wuv, ga, wo, g, wg, wu, wd, gf)


def _inproj_kernel(x_ref, gmix_ref, win_ref, gq_ref, wq_ref, gkv_ref, wa_ref, wb_ref,
                   c_ref, s1_ref, s2_ref, q_ref, ckv_ref, kpe_ref, glu_ref, *extra, absorbed):
    x = x_ref[...]
    h = _rms(x, gmix_ref[...]).astype(BF16)
    u = _dot(h, win_ref[...])
    c, s1, s2 = c_ref[...], s1_ref[...], s2_ref[...]

    qn = _rms(u[:, :Q_RANK], gq_ref[...]).astype(BF16)
    q_raw = _dot(qn, wq_ref[...])
    ckv = _rms(u[:, Q_RANK:KR_OFF], gkv_ref[...])
    ckv_ref[...] = ckv
    ckv_b = ckv.astype(BF16)
    kpe = _rope_tile(u[:, KR_OFF:GA_OFF], c, s1, s2)
    if absorbed:
        kpe_ref[...] = kpe[:, :ROPE_DIM]
    else:
        kpe_t = kpe.T[:ROPE_DIM]
        for pg in range(kpe_ref.shape[0]):
            kpe_ref[pg] = kpe_t[:, pg * PAGE_SIZE:(pg + 1) * PAGE_SIZE]
    glu_ref[...] = u[:, GA_OFF:GB_OFF] * jax.nn.sigmoid(u[:, GB_OFF:IN_COLS_PAD])

    if absorbed:
        (qabs_ref,) = extra
    else:
        k_ref, vt_ref = extra
        k_nope = _dot(ckv_b, wa_ref[...])
        vt_ref[0] = _dot_nt(wb_ref[...], ckv_b).astype(BF16)
    q_scale = SM_SCALE if absorbed else SM_SCALE * LOG2_E
    for hd in range(N_HEADS):
        sl = slice(hd * HEAD_PAD, (hd + 1) * HEAD_PAD)
        qh = (_rope_tile(q_raw[:, sl], c, s1, s2) * q_scale).astype(BF16)
        if absorbed:
            q_ref[:, sl] = qh.astype(F32)
            qabs_ref[:, hd * KV_RANK:(hd + 1) * KV_RANK] = (
                _dot(qh, wa_ref[hd]).astype(BF16).astype(F32))
        else:
            q_ref[hd] = qh
            k_ref[hd] = (k_nope[:, sl] + kpe).astype(BF16)


def _inproj(x, gmix, win, gq, wq, gkv, wa, wb, tables, absorbed):
    n = x.shape[0]
    tm = TOKEN_TILE
    row = lambda w: pl.BlockSpec((tm, w), lambda i: (i, 0))
    table_blocks = tables[0].shape[0] // tm
    table = pl.BlockSpec((tm, LANES), lambda i: (i % table_blocks, 0))
    heads_spec = pl.BlockSpec((N_HEADS, tm, HEAD_PAD), lambda i: (0, i, 0))
    heads_shape = jax.ShapeDtypeStruct((N_HEADS, n, HEAD_PAD), BF16)
    out_shape = [
        jax.ShapeDtypeStruct((n, D_HEADS_PAD), F32) if absorbed else heads_shape,
        jax.ShapeDtypeStruct((n, KV_RANK), F32),
        jax.ShapeDtypeStruct((n, ROPE_DIM) if absorbed else (n // PAGE_SIZE, ROPE_DIM, PAGE_SIZE),
                             F32),
        jax.ShapeDtypeStruct((n, CONV_DIM), F32),
    ]
    kpe_spec = row(ROPE_DIM) if absorbed else pl.BlockSpec(
        (tm // PAGE_SIZE, ROPE_DIM, PAGE_SIZE), lambda i: (i, 0, 0))
    out_specs = [row(D_HEADS_PAD) if absorbed else heads_spec, row(KV_RANK), kpe_spec,
                 row(CONV_DIM)]
    if absorbed:
        out_shape.append(jax.ShapeDtypeStruct((n, N_HEADS * KV_RANK), F32))
        out_specs.append(row(N_HEADS * KV_RANK))
    else:
        out_shape += [heads_shape, jax.ShapeDtypeStruct((n // tm, D_ATTN, tm), BF16)]
        out_specs += [heads_spec, pl.BlockSpec((1, D_ATTN, tm), lambda i: (i, 0, 0))]
    return pl.pallas_call(
        functools.partial(_inproj_kernel, absorbed=absorbed),
        out_shape=out_shape,
        grid=(n // tm,),
        in_specs=[row(D_MODEL), _const_spec((1, D_MODEL)), _const_spec(win.shape),
                  _const_spec((1, Q_RANK)), _const_spec(wq.shape), _const_spec((1, KV_RANK)),
                  _const_spec(wa.shape), _const_spec(wb.shape), table, table, table],
        out_specs=out_specs,
        compiler_params=_params(("parallel",)),
        name="inproj_absorbed" if absorbed else "inproj",
    )(x, gmix, win, gq, wq, gkv, wa, wb, *tables)


def _flash_kernel(q_ref, k_ref, vt_ref, o_ref, m_ref, acc_ref):
    t = ATTN_TILE
    half = t // 2
    qi = pl.program_id(2)
    heads = range(HEADS_PER_STEP)
    m_ref[...] = jnp.full(m_ref.shape, NEG_INF, F32)
    acc_ref[...] = jnp.zeros(acc_ref.shape, F32)

    def scores(j, h, n_keys, queries):
        k = k_ref[h, pl.ds(pl.multiple_of(j * t, t), n_keys), :]
        return _dot_nt(k, q_ref[h, queries, :])

    def update(j, h, s, n_keys, queries):
        vt = jnp.concatenate([vt_ref[j, h * V_DIM:(h + 1) * V_DIM, :n_keys],
                              jnp.ones((ONES_ROWS, n_keys), BF16)], axis=0)
        m = m_ref[h, :, queries]
        m_new = jnp.maximum(m, jnp.max(s, axis=0, keepdims=True))
        p = jnp.exp2(s - m_new).astype(BF16)
        acc_ref[h, :, queries] = jnp.exp2(m - m_new) * acc_ref[h, :, queries] + _dot(vt, p)
        m_ref[h, :, queries] = m_new

    def run(units, score_fn, update_fn):
        ahead = [score_fn(u) for u in units[:HEADS_AHEAD]]
        for i, u in enumerate(units):
            if i + HEADS_AHEAD < len(units):
                ahead.append(score_fn(units[i + HEADS_AHEAD]))
            update_fn(u, ahead[i])

    def full_tile(j, carry):
        everything = slice(0, t)
        run(list(heads), lambda h: scores(j, h, t, everything),
            lambda h, s: update(j, h, s, t, everything))
        return carry

    lax.fori_loop(0, qi, full_tile, 0)

    def diag_scores(u):
        h, part = u
        return scores(qi, h, (part + 1) * half, slice(part * half, (part + 1) * half))

    def diag_update(u, s):
        h, part = u
        n_keys = (part + 1) * half
        key_pos = lax.broadcasted_iota(jnp.int32, (n_keys, half), 0)
        query_pos = lax.broadcasted_iota(jnp.int32, (n_keys, half), 1) + part * half
        s = jnp.where(query_pos >= key_pos, s, NEG_INF)
        update(qi, h, s, n_keys, slice(part * half, (part + 1) * half))

    run([(h, part) for h in heads for part in range(2)], diag_scores, diag_update)
    for h in heads:
        acc = acc_ref[h]
        o_ref[:, h * V_DIM:(h + 1) * V_DIM] = (acc[:V_DIM] / acc[V_DIM:V_DIM + 1]).T


def _flash(q, k, vt, batch, seq):
    t = ATTN_TILE
    nq = seq // t
    rows = HEADS_PER_STEP * V_DIM
    return pl.pallas_call(
        _flash_kernel,
        out_shape=jax.ShapeDtypeStruct((batch * seq, D_ATTN), F32),
        grid=(batch, N_HEADS // HEADS_PER_STEP, nq),
        in_specs=[pl.BlockSpec((HEADS_PER_STEP, t, HEAD_PAD), lambda b, h, i: (h, b * nq + i, 0)),
                  pl.BlockSpec((HEADS_PER_STEP, seq, HEAD_PAD), lambda b, h, i: (h, b, 0)),
                  pl.BlockSpec((nq, rows, t), lambda b, h, i: (b, h, 0))],
        out_specs=pl.BlockSpec((t, rows), lambda b, h, i: (b * nq + i, h)),
        scratch_shapes=[pltpu.VMEM((HEADS_PER_STEP, 1, t), F32),
                        pltpu.VMEM((HEADS_PER_STEP, V_DIM + ONES_ROWS, t), F32)],
        compiler_params=_params(("parallel", "parallel", "arbitrary")),
        name="flash_prompt",
    )(q, k, vt)


def _paged_kernel(pt_ref, qa_ref, qp_ref, cn_ref, kn_ref, kv_hbm, kr_hbm, o_ref,
                  kvbuf, krbuf, sem, *, n_pages, n_new):
    b = pl.program_id(0)
    nb = pl.num_programs(0)
    slot = b % 2
    rows = n_new * N_HEADS

    def page_copies(bb, sl, j, off):
        page = pt_ref[bb * n_pages + j]
        return (pltpu.make_async_copy(kv_hbm.at[0, page],
                                      kvbuf.at[sl, pl.ds(off, PAGE_SIZE), :], sem.at[sl, 0]),
                pltpu.make_async_copy(kr_hbm.at[0, page], krbuf.at[sl, j], sem.at[sl, 1]))

    def wait_all(sl):
        pltpu.make_async_copy(kvbuf.at[sl], kvbuf.at[sl], sem.at[sl, 0]).wait()
        pltpu.make_async_copy(krbuf.at[sl], krbuf.at[sl], sem.at[sl, 1]).wait()

    @pl.when(b == 0)
    def _():
        def body(j, carry):
            for cp in page_copies(0, 0, j, pl.multiple_of(j * PAGE_SIZE, PAGE_SIZE)):
                cp.start()
            return carry
        lax.fori_loop(0, n_pages, body, 0)

    nxt = (b + 1) % nb
    for j in range(n_pages):
        for cp in page_copies(nxt, 1 - slot, j, j * PAGE_SIZE):
            cp.start()
    wait_all(slot)

    qa_all = qa_ref[0]
    qp_all = qp_ref[0]
    qa = jnp.concatenate([qa_all[:, h * KV_RANK:(h + 1) * KV_RANK] for h in range(N_HEADS)], axis=0)
    qp = jnp.concatenate([qp_all[:, h * HEAD_PAD:h * HEAD_PAD + ROPE_DIM]
                          for h in range(N_HEADS)], axis=0)

    cn = cn_ref[0]
    s = _dot_nt(qa, cn) + _dot_nt(qp, kn_ref[0])
    tq = lax.broadcasted_iota(jnp.int32, (rows, n_new), 0) % n_new
    tk = lax.broadcasted_iota(jnp.int32, (rows, n_new), 1)
    s = jnp.where(tq >= tk, s, NEG_INF)
    m = jnp.max(s, axis=-1, keepdims=True)
    p = jnp.exp(s - m)
    parts = [(m, jnp.sum(p, axis=-1, keepdims=True), _dot(p, cn))]

    n_chunks = n_pages * PAGE_SIZE // CHUNK_KEYS

    def chunk_kv(c):
        return kvbuf[slot, c * CHUNK_KEYS:(c + 1) * CHUNK_KEYS, :]

    def scores(c):
        pages = range(c * CHUNK_KEYS // PAGE_SIZE, (c + 1) * CHUNK_KEYS // PAGE_SIZE)
        krt = jnp.concatenate([krbuf[slot, p] for p in pages], axis=1)
        return _dot_nt(qa, chunk_kv(c)) + _dot(qp, krt)

    ahead = [scores(c) for c in range(min(SCORES_AHEAD, n_chunks))]
    for c in range(n_chunks):
        if c + SCORES_AHEAD < n_chunks:
            ahead.append(scores(c + SCORES_AHEAD))
        s = ahead[c]
        m = jnp.max(s, axis=-1, keepdims=True)
        p = jnp.exp(s - m)
        parts.append((m, jnp.sum(p, axis=-1, keepdims=True), _dot(p, chunk_kv(c))))

    m_all = functools.reduce(jnp.maximum, [m for m, _, _ in parts])
    l = jnp.zeros((rows, 1), F32)
    acc = jnp.zeros((rows, KV_RANK), F32)
    for m, l_c, acc_c in parts:
        w = jnp.exp(m - m_all)
        l = l + w * l_c
        acc = acc + w * acc_c
    o = acc / l
    for h in range(N_HEADS):
        o_ref[0, :, h * KV_RANK:(h + 1) * KV_RANK] = o[h * n_new:(h + 1) * n_new]

    @pl.when(b == nb - 1)
    def _():
        wait_all(1 - slot)


def _paged_attention(page_table, q_abs, q_rope, c_new, k_new, cache_kv, cache_kr_t):
    db, n_new = c_new.shape[0], c_new.shape[1]
    n_pages = page_table.shape[1]
    slot_keys = n_pages * PAGE_SIZE
    assert slot_keys % CHUNK_KEYS == 0 and n_new % SUBLANES == 0
    per_b = lambda w: pl.BlockSpec((1, n_new, w), lambda b, pt: (b, 0, 0))
    grid_spec = pltpu.PrefetchScalarGridSpec(
        num_scalar_prefetch=1,
        grid=(db,),
        in_specs=[per_b(N_HEADS * KV_RANK), per_b(D_HEADS_PAD), per_b(KV_RANK),
                  per_b(ROPE_DIM), pl.BlockSpec(memory_space=pl.ANY),
                  pl.BlockSpec(memory_space=pl.ANY)],
        out_specs=per_b(N_HEADS * KV_RANK),
        scratch_shapes=[pltpu.VMEM((2, slot_keys, KV_RANK), F32),
                        pltpu.VMEM((2, n_pages, ROPE_DIM, PAGE_SIZE), F32),
                        pltpu.SemaphoreType.DMA((2, 2))],
    )
    return pl.pallas_call(
        functools.partial(_paged_kernel, n_pages=n_pages, n_new=n_new),
        out_shape=jax.ShapeDtypeStruct((db, n_new, N_HEADS * KV_RANK), F32),
        grid_spec=grid_spec,
        compiler_params=_params(("arbitrary",)),
        name="paged_attention",
    )(page_table.reshape(-1), q_abs, q_rope, c_new, k_new, cache_kv, cache_kr_t)


EXT_HEAD = 32


def _conv_kernel(g_ref, st_ref, w_ref, bdw_ref, gcn_ref, bcn_ref, gout_ref, o_ref, ext_ref, sh_ref,
                 *, tt):
    ti = pl.program_id(1)
    pad = EXT_HEAD - CONV_STATE
    rb = min(CONV_ROWS, tt)
    gb = g_ref.shape[0]
    lane_tiles = [slice(c * LANES, (c + 1) * LANES) for c in range(CONV_DIM // LANES)]

    @pl.when(ti == 0)
    def _():
        ext_ref[:, :, 0:pad, :] = jnp.zeros((gb, len(lane_tiles), pad, LANES), F32)
        for c, cs in enumerate(lane_tiles):
            ext_ref[:, c, pad:EXT_HEAD, :] = st_ref[:, :, cs]

    @pl.when(ti > 0)
    def _():
        ext_ref[:, :, 0:EXT_HEAD, :] = ext_ref[:, :, tt:tt + EXT_HEAD, :]

    for c, cs in enumerate(lane_tiles):
        ext_ref[:, c, EXT_HEAD:EXT_HEAD + tt, :] = g_ref[:, :, cs]

    span = tt + EXT_HEAD - SUBLANES
    for r in range(1, SUBLANES):
        sh_ref[r - 1, :, :, 0:span, :] = ext_ref[:, :, r:r + span, :]

    for c, cs in enumerate(lane_tiles):
        taps = [w_ref[k:k + 1, cs] for k in range(CONV_WIDTH)]
        bias = bdw_ref[:, cs]

        def block(i, carry, c=c, cs=cs, taps=taps, bias=bias):
            r0 = pl.multiple_of(i * rb, rb)
            acc = jnp.zeros((gb, rb, LANES), F32) + bias
            for o in range(pad, EXT_HEAD + 1):
                a, r = divmod(o, SUBLANES)
                rows = pl.ds(r0 + SUBLANES * a, rb)
                src = ext_ref[:, c, rows, :] if r == 0 else sh_ref[r - 1, :, c, rows, :]
                acc = acc + src * taps[o - pad]
            o_ref[:, pl.ds(r0, rb), cs] = acc
            return carry

        lax.fori_loop(0, tt // rb, block, 0)
    conv = o_ref[...]
    mu = jnp.mean(conv, axis=-1, keepdims=True)
    d = conv - mu
    var = jnp.mean(d * d, axis=-1, keepdims=True)
    y = d * lax.rsqrt(var + EPS) * gcn_ref[...] + bcn_ref[...]
    y = y * jax.nn.sigmoid(y)
    o_ref[...] = _rms(y, gout_ref[...])


def _conv(glu, state, w_dw, b_dw, g_cn, b_cn, g_out, tt, gb):
    b, t = glu.shape[0], glu.shape[1]
    vec = _const_spec((1, CONV_DIM))
    return pl.pallas_call(
        functools.partial(_conv_kernel, tt=tt),
        out_shape=jax.ShapeDtypeStruct((b, t, CONV_DIM), F32),
        grid=(b // gb, t // tt),
        in_specs=[pl.BlockSpec((gb, tt, CONV_DIM), lambda i, j: (i, j, 0)),
                  pl.BlockSpec((None, gb, CONV_STATE, CONV_DIM), lambda i, j: (0, i, 0, 0)),
                  _const_spec(w_dw.shape), vec, vec, vec, vec],
        out_specs=pl.BlockSpec((gb, tt, CONV_DIM), lambda i, j: (i, j, 0)),
        scratch_shapes=[pltpu.VMEM((gb, CONV_DIM // LANES, EXT_HEAD + tt, LANES), F32),
                        pltpu.VMEM((SUBLANES - 1, gb, CONV_DIM // LANES,
                                    EXT_HEAD + tt - SUBLANES, LANES), F32)],
        compiler_params=_params(("parallel", "arbitrary")),
        name="conv_module",
    )(glu, state, w_dw, b_dw, g_cn, b_cn, g_out)


def _pad_heads(w, lo):
    d = w.shape[-1]
    w = jnp.pad(w, [(0, 0)] * (w.ndim - 1) + [(lo, HEAD_PAD - lo - d)])
    return w.reshape(*w.shape[:-2], D_HEADS_PAD)


def _rope_tables(pos):
    inv_freq = ROPE_THETA ** (-jnp.arange(0, ROPE_DIM, 2, dtype=F32) / ROPE_DIM)
    ang = pos.astype(F32)[:, None] * inv_freq[None, :]
    cos, sin = jnp.cos(ang), jnp.sin(ang)
    rows = pos.shape[0]
    zeros = lambda w: jnp.zeros((rows, w), F32)
    c = jnp.concatenate([cos, cos, jnp.ones((rows, LANES - ROPE_DIM), F32)], axis=1)
    s1 = jnp.concatenate([zeros(HALF_ROPE), sin, zeros(LANES - ROPE_DIM)], axis=1)
    s2 = jnp.concatenate([-sin, zeros(LANES - HALF_ROPE)], axis=1)
    return c, s1, s2


def kernel(x_prompt, x_sample, cache_kv_latent, cache_k_rope, state_conv, page_table, g_ffn1, w1_gate, w1_up, w1_down, g_mix, w_in, g_q, w_q_b, g_kv, w_kv_b, w_dw, b_dw, g_cn, b_cn, g_out_attn, g_out_conv, w_out, g_ffn2, w2_gate, w2_up, w2_down, g_final):
    depth = g_ffn1.shape[0]
    assert depth == 1
    batch, seq, _ = x_prompt.shape
    db, t_new, _ = x_sample.shape
    n_pages = page_table.shape[1]
    past_len = n_pages * PAGE_SIZE
    assert seq % TOKEN_TILE == 0 and TOKEN_TILE % t_new == 0 and (db * t_new) % TOKEN_TILE == 0
    l = 0
    row = lambda v: v.reshape(1, -1)

    win = jnp.concatenate(
        [w_in[l][:, :KR_OFF + ROPE_DIM], jnp.zeros((D_MODEL, LANES - ROPE_DIM), F32),
         w_in[l][:, KR_OFF + ROPE_DIM:]], axis=1).astype(BF16)
    wq_heads = jnp.concatenate([w_q_b[l][..., QK_NOPE:], w_q_b[l][..., :QK_NOPE]], axis=-1)
    wq = _pad_heads(wq_heads, 0).astype(BF16)
    w_uk = w_kv_b[l][..., :QK_NOPE]
    w_uv = w_kv_b[l][..., QK_NOPE:]
    wk_exp = _pad_heads(w_uk, ROPE_DIM).astype(BF16)
    wv_t = w_uv.reshape(KV_RANK, D_ATTN).T.astype(BF16)
    wk_abs = jnp.pad(jnp.transpose(w_uk, (1, 2, 0)),
                     ((0, 0), (ROPE_DIM, HEAD_PAD - ROPE_DIM - QK_NOPE), (0, 0))).astype(BF16)
    eye = jnp.eye(N_HEADS, dtype=F32)
    wuv_bd = (jnp.transpose(w_uv, (1, 0, 2))[:, :, None, :] * eye[:, None, :, None])
    wuv_bd = wuv_bd.reshape(N_HEADS * KV_RANK, D_ATTN).astype(BF16)
    wo = w_out[l].astype(BF16)
    w1g, w1u, w1d = w1_gate[l].astype(BF16), w1_up[l].astype(BF16), w1_down[l].astype(BF16)
    w2g, w2u, w2d = w2_gate[l].astype(BF16), w2_up[l].astype(BF16), w2_down[l].astype(BF16)
    unused = jnp.zeros((SUBLANES, LANES), BF16)

    def stream(x, tables, conv_state, conv_batch, absorbed, attend):
        nb, nt = x.shape[0], x.shape[1]
        n = nb * nt
        x1 = _ffn(x.reshape(n, D_MODEL), row(g_ffn1[l]), w1g, w1u, w1d)
        wa, wb = (wk_abs, unused) if absorbed else (wk_exp, wv_t)
        q, ckv, kpe, glu, *extra = _inproj(x1, row(g_mix[l]), win, row(g_q[l]), wq, row(g_kv[l]),
                                           wa, wb, tables, absorbed)
        attn = attend(q, ckv, kpe, *extra)
        glu3 = glu.reshape(nb, nt, CONV_DIM)
        conv = _conv(glu3, conv_state, w_dw[l], row(b_dw[l]), row(g_cn[l]), row(b_cn[l]),
                     row(g_out_conv[l]), min(CONV_TILE, nt), conv_batch).reshape(n, CONV_DIM)
        y = _merge_ffn(x1, attn, conv, wuv_bd if absorbed else unused, row(g_out_attn[l]),
                       wo, row(g_ffn2[l]), w2g, w2u, w2d, row(g_final), absorbed)
        new_conv = jnp.concatenate([conv_state[0], glu3], axis=1)[:, -CONV_STATE:]
        return y.reshape(nb, nt, D_MODEL), ckv, kpe, new_conv

    def attend_prompt(q, ckv, kpe, k, vt):
        return _flash(q, k, vt, batch, seq)

    y_p, ckv_p, kpe_p, cv_p = stream(
        x_prompt, _rope_tables(jnp.arange(seq)),
        jnp.zeros((1, batch, CONV_STATE, CONV_DIM), F32), 1, False, attend_prompt)

    cache_kr_t = jnp.swapaxes(cache_k_rope, 2, 3)

    def attend_sample(q, ckv, kpe, q_abs):
        o_lat = _paged_attention(
            page_table, q_abs.reshape(db, t_new, N_HEADS * KV_RANK),
            q.reshape(db, t_new, D_HEADS_PAD),
            ckv.reshape(db, t_new, KV_RANK), kpe.reshape(db, t_new, ROPE_DIM),
            cache_kv_latent, cache_kr_t)
        return o_lat.reshape(db * t_new, N_HEADS * KV_RANK)

    pos_s = jnp.tile(past_len + jnp.arange(t_new), TOKEN_TILE // t_new)
    y_s, ckv_s, kpe_s, cv_s = stream(x_sample, _rope_tables(pos_s), state_conv,
                                     CONV_BATCH, True, attend_sample)

    n_pg = seq // PAGE_SIZE
    return (y_p, y_s,
            ckv_p.reshape(1, batch, n_pg, PAGE_SIZE, KV_RANK),
            jnp.swapaxes(kpe_p.reshape(1, batch, n_pg, ROPE_DIM, PAGE_SIZE), 3, 4),
            cv_p[None],
            ckv_s.reshape(1, db, t_new, KV_RANK),
            kpe_s.reshape(1, db, t_new, ROPE_DIM),
            cv_s[None])
```

```python
import functools

import jax
import jax.numpy as jnp
from jax import lax
from jax.experimental import pallas as pl
from jax.experimental.pallas import tpu as pltpu

D_MODEL = 1024
N_HEADS = 8
QK_NOPE = 64
ROPE_DIM = 32
HALF_ROPE = ROPE_DIM // 2
V_DIM = 64
Q_RANK = 384
KV_RANK = 256
CONV_DIM = 512
CONV_WIDTH = 31
CONV_STATE = CONV_WIDTH - 1
D_ATTN = N_HEADS * V_DIM
D_FF = 2816
PAGE_SIZE = 128
ROPE_THETA = 10000.0
EPS = 1e-6
SM_SCALE = (QK_NOPE + ROPE_DIM) ** -0.5
LOG2_E = 1.4426950408889634
NEG_INF = -1e30

LANES = 128
SUBLANES = 8
HEAD_PAD = LANES
D_HEADS_PAD = N_HEADS * HEAD_PAD
KR_OFF = Q_RANK + KV_RANK
GA_OFF = KR_OFF + LANES
GB_OFF = GA_OFF + CONV_DIM
IN_COLS_PAD = GB_OFF + CONV_DIM
VMEM_LIMIT = 56 * 1024 * 1024
N_DMA_THREADS = 2

TOKEN_TILE = 512
ATTN_TILE = TOKEN_TILE
ONES_ROWS = 16
HEADS_AHEAD = 2
HEADS_PER_STEP = 8
CONV_TILE = 512
CONV_ROWS = 128
CONV_BATCH = 16
CHUNK_KEYS = 2048
SCORES_AHEAD = 3

BF16 = jnp.bfloat16
F32 = jnp.float32


def _dot(a, b):
    return jnp.dot(a, b, preferred_element_type=F32)


def _dot_nt(a, b):
    return lax.dot_general(a, b, (((1,), (1,)), ((), ())), preferred_element_type=F32)


def _rms(x, g, axis=-1):
    ms = jnp.mean(x * x, axis=axis, keepdims=True)
    return x * lax.rsqrt(ms + EPS) * g


def _rope_tile(blk, c, s1, s2):
    return (blk * c + pltpu.roll(blk, HALF_ROPE, 1) * s1
            + pltpu.roll(blk, LANES - HALF_ROPE, 1) * s2)


def _const_spec(shape):
    nd = len(shape)
    return pl.BlockSpec(shape, lambda *_: (0,) * nd, pipeline_mode=pl.Buffered(1))


def _params(sem):
    return pltpu.CompilerParams(dimension_semantics=sem, vmem_limit_bytes=VMEM_LIMIT)


def _half_ffn(x, g_ref, wg_ref, wu_ref, wd_ref):
    h = _rms(x, g_ref[...]).astype(BF16)
    a = _dot(h, wg_ref[...])
    a = a * jax.nn.sigmoid(a) * _dot(h, wu_ref[...])
    return x + 0.5 * _dot(a.astype(BF16), wd_ref[...])


def _ffn_kernel(x_ref, g_ref, wg_ref, wu_ref, wd_ref, o_ref):
    o_ref[...] = _half_ffn(x_ref[...], g_ref, wg_ref, wu_ref, wd_ref)


def _ffn_weight_specs():
    return [_const_spec((1, D_MODEL)), _const_spec((D_MODEL, D_FF)),
            _const_spec((D_MODEL, D_FF)), _const_spec((D_FF, D_MODEL))]


def _ffn(x, g, wg, wu, wd):
    n = x.shape[0]
    tm = TOKEN_TILE
    tok = pl.BlockSpec((tm, D_MODEL), lambda i: (i, 0))
    return pl.pallas_call(
        _ffn_kernel,
        out_shape=jax.ShapeDtypeStruct((n, D_MODEL), F32),
        grid=(n // tm,),
        in_specs=[tok] + _ffn_weight_specs(),
        out_specs=tok,
        compiler_params=_params(("parallel",)),
        name="ffn",
    )(x, g, wg, wu, wd)


def _merge_ffn_kernel(x_ref, a_ref, cv_ref, wuv_ref, ga_ref, wo_ref,
                      g_ref, wg_ref, wu_ref, wd_ref, gf_ref, o_ref, *, absorbed):
    if absorbed:
        attn = _dot(a_ref[...].astype(BF16), wuv_ref[...])
    else:
        attn = a_ref[...]
    mix = jnp.concatenate([_rms(attn, ga_ref[...]).astype(BF16), cv_ref[...].astype(BF16)], axis=1)
    x = x_ref[...] + _dot(mix, wo_ref[...])
    o_ref[...] = _rms(_half_ffn(x, g_ref, wg_ref, wu_ref, wd_ref), gf_ref[...])


def _merge_ffn(x, attn, conv, wuv, ga, wo, g, wg, wu, wd, gf, absorbed):
    n = x.shape[0]
    tm = TOKEN_TILE
    row = lambda w: pl.BlockSpec((tm, w), lambda i: (i, 0))
    return pl.pallas_call(
        functools.partial(_merge_ffn_kernel, absorbed=absorbed),
        out_shape=jax.ShapeDtypeStruct((n, D_MODEL), F32),
        grid=(n // tm,),
        in_specs=[row(D_MODEL), row(attn.shape[1]), row(CONV_DIM), _const_spec(wuv.shape),
                  _const_spec((1, D_ATTN)), _const_spec(wo.shape)]
                 + _ffn_weight_specs() + [_const_spec((1, D_MODEL))],
        out_specs=row(D_MODEL),
        compiler_params=_params(("parallel",)),
        name="merge_ffn_absorbed" if absorbed else "merge_ffn",
    )(x, attn, conv, wuv, ga, wo, g, wg, wu, wd, gf)


def _inproj_kernel(x_ref, gmix_ref, win_ref, gq_ref, wq_ref, gkv_ref, wa_ref, wb_ref,
                   c_ref, s1_ref, s2_ref, q_ref, ckv_ref, kpe_ref, glu_ref, *extra, absorbed):
    x = x_ref[...]
    h = _rms(x, gmix_ref[...]).astype(BF16)
    u = _dot(h, win_ref[...])
    c, s1, s2 = c_ref[...], s1_ref[...], s2_ref[...]

    qn = _rms(u[:, :Q_RANK], gq_ref[...]).astype(BF16)
    q_raw = _dot(qn, wq_ref[...])
    ckv = _rms(u[:, Q_RANK:KR_OFF], gkv_ref[...])
    ckv_ref[...] = ckv
    ckv_b = ckv.astype(BF16)
    kpe = _rope_tile(u[:, KR_OFF:GA_OFF], c, s1, s2)
    if absorbed:
        kpe_ref[...] = kpe[:, :ROPE_DIM]
    else:
        kpe_t = kpe.T[:ROPE_DIM]
        for pg in range(kpe_ref.shape[0]):
            kpe_ref[pg] = kpe_t[:, pg * PAGE_SIZE:(pg + 1) * PAGE_SIZE]
    glu_ref[...] = u[:, GA_OFF:GB_OFF] * jax.nn.sigmoid(u[:, GB_OFF:IN_COLS_PAD])

    if absorbed:
        (qabs_ref,) = extra
    else:
        k_ref, vt_ref = extra
        k_nope = _dot(ckv_b, wa_ref[...])
        vt_ref[0] = _dot_nt(wb_ref[...], ckv_b).astype(BF16)
    q_scale = SM_SCALE if absorbed else SM_SCALE * LOG2_E
    for hd in range(N_HEADS):
        sl = slice(hd * HEAD_PAD, (hd + 1) * HEAD_PAD)
        qh = (_rope_tile(q_raw[:, sl], c, s1, s2) * q_scale).astype(BF16)
        if absorbed:
            q_ref[:, sl] = qh.astype(F32)
            qabs_ref[:, hd * KV_RANK:(hd + 1) * KV_RANK] = (
                _dot(qh, wa_ref[hd]).astype(BF16).astype(F32))
        else:
            q_ref[hd] = qh
            k_ref[hd] = (k_nope[:, sl] + kpe).astype(BF16)


def _inproj(x, gmix, win, gq, wq, gkv, wa, wb, tables, absorbed):
    n = x.shape[0]
    tm = TOKEN_TILE
    row = lambda w: pl.BlockSpec((tm, w), lambda i: (i, 0))
    table_blocks = tables[0].shape[0] // tm
    table = pl.BlockSpec((tm, LANES), lambda i: (i % table_blocks, 0))
    heads_spec = pl.BlockSpec((N_HEADS, tm, HEAD_PAD), lambda i: (0, i, 0))
    heads_shape = jax.ShapeDtypeStruct((N_HEADS, n, HEAD_PAD), BF16)
    out_shape = [
        jax.ShapeDtypeStruct((n, D_HEADS_PAD), F32) if absorbed else heads_shape,
        jax.ShapeDtypeStruct((n, KV_RANK), F32),
        jax.ShapeDtypeStruct((n, ROPE_DIM) if absorbed else (n // PAGE_SIZE, ROPE_DIM, PAGE_SIZE),
                             F32),
        jax.ShapeDtypeStruct((n, CONV_DIM), F32),
    ]
    kpe_spec = row(ROPE_DIM) if absorbed else pl.BlockSpec(
        (tm // PAGE_SIZE, ROPE_DIM, PAGE_SIZE), lambda i: (i, 0, 0))
    out_specs = [row(D_HEADS_PAD) if absorbed else heads_spec, row(KV_RANK), kpe_spec,
                 row(CONV_DIM)]
    if absorbed:
        out_shape.append(jax.ShapeDtypeStruct((n, N_HEADS * KV_RANK), F32))
        out_specs.append(row(N_HEADS * KV_RANK))
    else:
        out_shape += [heads_shape, jax.ShapeDtypeStruct((n // tm, D_ATTN, tm), BF16)]
        out_specs += [heads_spec, pl.BlockSpec((1, D_ATTN, tm), lambda i: (i, 0, 0))]
    return pl.pallas_call(
        functools.partial(_inproj_kernel, absorbed=absorbed),
        out_shape=out_shape,
        grid=(n // tm,),
        in_specs=[row(D_MODEL), _const_spec((1, D_MODEL)), _const_spec(win.shape),
                  _const_spec((1, Q_RANK)), _const_spec(wq.shape), _const_spec((1, KV_RANK)),
                  _const_spec(wa.shape), _const_spec(wb.shape), table, table, table],
        out_specs=out_specs,
        compiler_params=_params(("parallel",)),
        name="inproj_absorbed" if absorbed else "inproj",
    )(x, gmix, win, gq, wq, gkv, wa, wb, *tables)


def _flash_kernel(q_ref, k_ref, vt_ref, o_ref, m_ref, acc_ref):
    t = ATTN_TILE
    half = t // 2
    qi = pl.program_id(2)
    heads = range(HEADS_PER_STEP)
    m_ref[...] = jnp.full(m_ref.shape, NEG_INF, F32)
    acc_ref[...] = jnp.zeros(acc_ref.shape, F32)

    def scores(j, h, n_keys, queries):
        k = k_ref[h, pl.ds(pl.multiple_of(j * t, t), n_keys), :]
        return _dot_nt(k, q_ref[h, queries, :])

    def update(j, h, s, n_keys, queries):
        vt = jnp.concatenate([vt_ref[j, h * V_DIM:(h + 1) * V_DIM, :n_keys],
                              jnp.ones((ONES_ROWS, n_keys), BF16)], axis=0)
        m = m_ref[h, :, queries]
        m_new = jnp.maximum(m, jnp.max(s, axis=0, keepdims=True))
        p = jnp.exp2(s - m_new).astype(BF16)
        acc_ref[h, :, queries] = jnp.exp2(m - m_new) * acc_ref[h, :, queries] + _dot(vt, p)
        m_ref[h, :, queries] = m_new

    def run(units, score_fn, update_fn):
        ahead = [score_fn(u) for u in units[:HEADS_AHEAD]]
        for i, u in enumerate(units):
            if i + HEADS_AHEAD < len(units):
                ahead.append(score_fn(units[i + HEADS_AHEAD]))
            update_fn(u, ahead[i])

    def full_tile(j, carry):
        everything = slice(0, t)
        run(list(heads), lambda h: scores(j, h, t, everything),
            lambda h, s: update(j, h, s, t, everything))
        return carry

    lax.fori_loop(0, qi, full_tile, 0)

    def diag_scores(u):
        h, part = u
        return scores(qi, h, (part + 1) * half, slice(part * half, (part + 1) * half))

    def diag_update(u, s):
        h, part = u
        n_keys = (part + 1) * half
        key_pos = lax.broadcasted_iota(jnp.int32, (n_keys, half), 0)
        query_pos = lax.broadcasted_iota(jnp.int32, (n_keys, half), 1) + part * half
        s = jnp.where(query_pos >= key_pos, s, NEG_INF)
        update(qi, h, s, n_keys, slice(part * half, (part + 1) * half))

    run([(h, part) for h in heads for part in range(2)], diag_scores, diag_update)
    for h in heads:
        acc = acc_ref[h]
        o_ref[:, h * V_DIM:(h + 1) * V_DIM] = (acc[:V_DIM] / acc[V_DIM:V_DIM + 1]).T


def _flash(q, k, vt, batch, seq):
    t = ATTN_TILE
    nq = seq // t
    rows = HEADS_PER_STEP * V_DIM
    return pl.pallas_call(
        _flash_kernel,
        out_shape=jax.ShapeDtypeStruct((batch * seq, D_ATTN), F32),
        grid=(batch, N_HEADS // HEADS_PER_STEP, nq),
        in_specs=[pl.BlockSpec((HEADS_PER_STEP, t, HEAD_PAD), lambda b, h, i: (h, b * nq + i, 0)),
                  pl.BlockSpec((HEADS_PER_STEP, seq, HEAD_PAD), lambda b, h, i: (h, b, 0)),
                  pl.BlockSpec((nq, rows, t), lambda b, h, i: (b, h, 0))],
        out_specs=pl.BlockSpec((t, rows), lambda b, h, i: (b * nq + i, h)),
        scratch_shapes=[pltpu.VMEM((HEADS_PER_STEP, 1, t), F32),
                        pltpu.VMEM((HEADS_PER_STEP, V_DIM + ONES_ROWS, t), F32)],
        compiler_params=_params(("parallel", "parallel", "arbitrary")),
        name="flash_prompt",
    )(q, k, vt)


def _paged_kernel(pt_ref, qa_ref, qp_ref, cn_ref, kn_ref, kv_hbm, kr_hbm, o_ref,
                  kvbuf, krbuf, sem, *, n_pages, n_new):
    b = pl.program_id(0)
    nb = pl.num_programs(0)
    slot = b % 2
    rows = n_new * N_HEADS

    def page_copies(bb, sl, j, off):
        page = pt_ref[bb * n_pages + j]
        return (pltpu.make_async_copy(kv_hbm.at[0, page],
                                      kvbuf.at[sl, pl.ds(off, PAGE_SIZE), :], sem.at[sl, 0]),
                pltpu.make_async_copy(kr_hbm.at[0, page], krbuf.at[sl, j], sem.at[sl, 1]))

    def wait_all(sl):
        pltpu.make_async_copy(kvbuf.at[sl], kvbuf.at[sl], sem.at[sl, 0]).wait()
        pltpu.make_async_copy(krbuf.at[sl], krbuf.at[sl], sem.at[sl, 1]).wait()

    @pl.when(b == 0)
    def _():
        def body(j, carry):
            for cp in page_copies(0, 0, j, pl.multiple_of(j * PAGE_SIZE, PAGE_SIZE)):
                cp.start()
            return carry
        lax.fori_loop(0, n_pages, body, 0)

    nxt = (b + 1) % nb
    for j in range(n_pages):
        for i, cp in enumerate(page_copies(nxt, 1 - slot, j, j * PAGE_SIZE)):
            cp.start(priority=(i + j) % N_DMA_THREADS)
    wait_all(slot)

    qa_all = qa_ref[0]
    qp_all = qp_ref[0]
    qa = jnp.concatenate([qa_all[:, h * KV_RANK:(h + 1) * KV_RANK] for h in range(N_HEADS)], axis=0)
    qp = jnp.concatenate([qp_all[:, h * HEAD_PAD:h * HEAD_PAD + ROPE_DIM]
                          for h in range(N_HEADS)], axis=0)

    cn = cn_ref[0]
    s = _dot_nt(qa, cn) + _dot_nt(qp, kn_ref[0])
    tq = lax.broadcasted_iota(jnp.int32, (rows, n_new), 0) % n_new
    tk = lax.broadcasted_iota(jnp.int32, (rows, n_new), 1)
    s = jnp.where(tq >= tk, s, NEG_INF)
    m = jnp.max(s, axis=-1, keepdims=True)
    p = jnp.exp(s - m)
    parts = [(m, jnp.sum(p, axis=-1, keepdims=True), _dot(p, cn))]

    n_chunks = n_pages * PAGE_SIZE // CHUNK_KEYS

    def chunk_kv(c):
        return kvbuf[slot, c * CHUNK_KEYS:(c + 1) * CHUNK_KEYS, :]

    def scores(c):
        pages = range(c * CHUNK_KEYS // PAGE_SIZE, (c + 1) * CHUNK_KEYS // PAGE_SIZE)
        krt = jnp.concatenate([krbuf[slot, p] for p in pages], axis=1)
        return _dot_nt(qa, chunk_kv(c)) + _dot(qp, krt)

    ahead = [scores(c) for c in range(min(SCORES_AHEAD, n_chunks))]
    for c in range(n_chunks):
        if c + SCORES_AHEAD < n_chunks:
            ahead.append(scores(c + SCORES_AHEAD))
        s = ahead[c]
        m = jnp.max(s, axis=-1, keepdims=True)
        p = jnp.exp(s - m)
        parts.append((m, jnp.sum(p, axis=-1, keepdims=True), _dot(p, chunk_kv(c))))

    m_all = functools.reduce(jnp.maximum, [m for m, _, _ in parts])
    l = jnp.zeros((rows, 1), F32)
    acc = jnp.zeros((rows, KV_RANK), F32)
    for m, l_c, acc_c in parts:
        w = jnp.exp(m - m_all)
        l = l + w * l_c
        acc = acc + w * acc_c
    o = acc / l
    for h in range(N_HEADS):
        o_ref[0, :, h * KV_RANK:(h + 1) * KV_RANK] = o[h * n_new:(h + 1) * n_new]

    @pl.when(b == nb - 1)
    def _():
        wait_all(1 - slot)


def _paged_attention(page_table, q_abs, q_rope, c_new, k_new, cache_kv, cache_kr_t):
    db, n_new = c_new.shape[0], c_new.shape[1]
    n_pages = page_table.shape[1]
    slot_keys = n_pages * PAGE_SIZE
    assert slot_keys % CHUNK_KEYS == 0 and n_new % SUBLANES == 0
    per_b = lambda w: pl.BlockSpec((1, n_new, w), lambda b, pt: (b, 0, 0))
    grid_spec = pltpu.PrefetchScalarGridSpec(
        num_scalar_prefetch=1,
        grid=(db,),
        in_specs=[per_b(N_HEADS * KV_RANK), per_b(D_HEADS_PAD), per_b(KV_RANK),
                  per_b(ROPE_DIM), pl.BlockSpec(memory_space=pl.ANY),
                  pl.BlockSpec(memory_space=pl.ANY)],
        out_specs=per_b(N_HEADS * KV_RANK),
        scratch_shapes=[pltpu.VMEM((2, slot_keys, KV_RANK), F32),
                        pltpu.VMEM((2, n_pages, ROPE_DIM, PAGE_SIZE), F32),
                        pltpu.SemaphoreType.DMA((2, 2))],
    )
    return pl.pallas_call(
        functools.partial(_paged_kernel, n_pages=n_pages, n_new=n_new),
        out_shape=jax.ShapeDtypeStruct((db, n_new, N_HEADS * KV_RANK), F32),
        grid_spec=grid_spec,
        compiler_params=_params(("arbitrary",)),
        name="paged_attention",
    )(page_table.reshape(-1), q_abs, q_rope, c_new, k_new, cache_kv, cache_kr_t)


EXT_HEAD = 32


def _conv_kernel(g_ref, st_ref, w_ref, bdw_ref, gcn_ref, bcn_ref, gout_ref, o_ref, ext_ref, sh_ref,
                 *, tt):
    ti = pl.program_id(1)
    pad = EXT_HEAD - CONV_STATE
    rb = min(CONV_ROWS, tt)
    gb = g_ref.shape[0]
    lane_tiles = [slice(c * LANES, (c + 1) * LANES) for c in range(CONV_DIM // LANES)]

    @pl.when(ti == 0)
    def _():
        ext_ref[:, :, 0:pad, :] = jnp.zeros((gb, len(lane_tiles), pad, LANES), F32)
        for c, cs in enumerate(lane_tiles):
            ext_ref[:, c, pad:EXT_HEAD, :] = st_ref[:, :, cs]

    @pl.when(ti > 0)
    def _():
        ext_ref[:, :, 0:EXT_HEAD, :] = ext_ref[:, :, tt:tt + EXT_HEAD, :]

    for c, cs in enumerate(lane_tiles):
        ext_ref[:, c, EXT_HEAD:EXT_HEAD + tt, :] = g_ref[:, :, cs]

    span = tt + EXT_HEAD - SUBLANES
    for r in range(1, SUBLANES):
        sh_ref[r - 1, :, :, 0:span, :] = ext_ref[:, :, r:r + span, :]

    for c, cs in enumerate(lane_tiles):
        taps = [w_ref[k:k + 1, cs] for k in range(CONV_WIDTH)]
        bias = bdw_ref[:, cs]

        def block(i, carry, c=c, cs=cs, taps=taps, bias=bias):
            r0 = pl.multiple_of(i * rb, rb)
            acc = jnp.zeros((gb, rb, LANES), F32) + bias
            for o in range(pad, EXT_HEAD + 1):
                a, r = divmod(o, SUBLANES)
                rows = pl.ds(r0 + SUBLANES * a, rb)
                src = ext_ref[:, c, rows, :] if r == 0 else sh_ref[r - 1, :, c, rows, :]
                acc = acc + src * taps[o - pad]
            o_ref[:, pl.ds(r0, rb), cs] = acc
            return carry

        lax.fori_loop(0, tt // rb, block, 0)
    conv = o_ref[...]
    mu = jnp.mean(conv, axis=-1, keepdims=True)
    d = conv - mu
    var = jnp.mean(d * d, axis=-1, keepdims=True)
    y = d * lax.rsqrt(var + EPS) * gcn_ref[...] + bcn_ref[...]
    y = y * jax.nn.sigmoid(y)
    o_ref[...] = _rms(y, gout_ref[...])


def _conv(glu, state, w_dw, b_dw, g_cn, b_cn, g_out, tt, gb):
    b, t = glu.shape[0], glu.shape[1]
    vec = _const_spec((1, CONV_DIM))
    return pl.pallas_call(
        functools.partial(_conv_kernel, tt=tt),
        out_shape=jax.ShapeDtypeStruct((b, t, CONV_DIM), F32),
        grid=(b // gb, t // tt),
        in_specs=[pl.BlockSpec((gb, tt, CONV_DIM), lambda i, j: (i, j, 0)),
                  pl.BlockSpec((None, gb, CONV_STATE, CONV_DIM), lambda i, j: (0, i, 0, 0)),
                  _const_spec(w_dw.shape), vec, vec, vec, vec],
        out_specs=pl.BlockSpec((gb, tt, CONV_DIM), lambda i, j: (i, j, 0)),
        scratch_shapes=[pltpu.VMEM((gb, CONV_DIM // LANES, EXT_HEAD + tt, LANES), F32),
                        pltpu.VMEM((SUBLANES - 1, gb, CONV_DIM // LANES,
                                    EXT_HEAD + tt - SUBLANES, LANES), F32)],
        compiler_params=_params(("parallel", "arbitrary")),
        name="conv_module",
    )(glu, state, w_dw, b_dw, g_cn, b_cn, g_out)


def _pad_heads(w, lo):
    d = w.shape[-1]
    w = jnp.pad(w, [(0, 0)] * (w.ndim - 1) + [(lo, HEAD_PAD - lo - d)])
    return w.reshape(*w.shape[:-2], D_HEADS_PAD)


def _rope_tables(pos):
    inv_freq = ROPE_THETA ** (-jnp.arange(0, ROPE_DIM, 2, dtype=F32) / ROPE_DIM)
    ang = pos.astype(F32)[:, None] * inv_freq[None, :]
    cos, sin = jnp.cos(ang), jnp.sin(ang)
    rows = pos.shape[0]
    zeros = lambda w: jnp.zeros((rows, w), F32)
    c = jnp.concatenate([cos, cos, jnp.ones((rows, LANES - ROPE_DIM), F32)], axis=1)
    s1 = jnp.concatenate([zeros(HALF_ROPE), sin, zeros(LANES - ROPE_DIM)], axis=1)
    s2 = jnp.concatenate([-sin, zeros(LANES - HALF_ROPE)], axis=1)
    return c, s1, s2


def kernel(x_prompt, x_sample, cache_kv_latent, cache_k_rope, state_conv, page_table, g_ffn1, w1_gate, w1_up, w1_down, g_mix, w_in, g_q, w_q_b, g_kv, w_kv_b, w_dw, b_dw, g_cn, b_cn, g_out_attn, g_out_conv, w_out, g_ffn2, w2_gate, w2_up, w2_down, g_final):
    depth = g_ffn1.shape[0]
    assert depth == 1
    batch, seq, _ = x_prompt.shape
    db, t_new, _ = x_sample.shape
    n_pages = page_table.shape[1]
    past_len = n_pages * PAGE_SIZE
    assert seq % TOKEN_TILE == 0 and TOKEN_TILE % t_new == 0 and (db * t_new) % TOKEN_TILE == 0
    l = 0
    row = lambda v: v.reshape(1, -1)

    win = jnp.concatenate(
        [w_in[l][:, :KR_OFF + ROPE_DIM], jnp.zeros((D_MODEL, LANES - ROPE_DIM), F32),
         w_in[l][:, KR_OFF + ROPE_DIM:]], axis=1).astype(BF16)
    wq_heads = jnp.concatenate([w_q_b[l][..., QK_NOPE:], w_q_b[l][..., :QK_NOPE]], axis=-1)
    wq = _pad_heads(wq_heads, 0).astype(BF16)
    w_uk = w_kv_b[l][..., :QK_NOPE]
    w_uv = w_kv_b[l][..., QK_NOPE:]
    wk_exp = _pad_heads(w_uk, ROPE_DIM).astype(BF16)
    wv_t = w_uv.reshape(KV_RANK, D_ATTN).T.astype(BF16)
    wk_abs = jnp.pad(jnp.transpose(w_uk, (1, 2, 0)),
                     ((0, 0), (ROPE_DIM, HEAD_PAD - ROPE_DIM - QK_NOPE), (0, 0))).astype(BF16)
    eye = jnp.eye(N_HEADS, dtype=F32)
    wuv_bd = (jnp.transpose(w_uv, (1, 0, 2))[:, :, None, :] * eye[:, None, :, None])
    wuv_bd = wuv_bd.reshape(N_HEADS * KV_RANK, D_ATTN).astype(BF16)
    wo = w_out[l].astype(BF16)
    w1g, w1u, w1d = w1_gate[l].astype(BF16), w1_up[l].astype(BF16), w1_down[l].astype(BF16)
    w2g, w2u, w2d = w2_gate[l].astype(BF16), w2_up[l].astype(BF16), w2_down[l].astype(BF16)
    unused = jnp.zeros((SUBLANES, LANES), BF16)

    def stream(x, tables, conv_state, conv_batch, absorbed, attend):
        nb, nt = x.shape[0], x.shape[1]
        n = nb * nt
        x1 = _ffn(x.reshape(n, D_MODEL), row(g_ffn1[l]), w1g, w1u, w1d)
        wa, wb = (wk_abs, unused) if absorbed else (wk_exp, wv_t)
        q, ckv, kpe, glu, *extra = _inproj(x1, row(g_mix[l]), win, row(g_q[l]), wq, row(g_kv[l]),
                                           wa, wb, tables, absorbed)
        attn = attend(q, ckv, kpe, *extra)
        glu3 = glu.reshape(nb, nt, CONV_DIM)
        conv = _conv(glu3, conv_state, w_dw[l], row(b_dw[l]), row(g_cn[l]), row(b_cn[l]),
                     row(g_out_conv[l]), min(CONV_TILE, nt), conv_batch).reshape(n, CONV_DIM)
        y = _merge_ffn(x1, attn, conv, wuv_bd if absorbed else unused, row(g_out_attn[l]),
                       wo, row(g_ffn2[l]), w2g, w2u, w2d, row(g_final), absorbed)
        new_conv = jnp.concatenate([conv_state[0], glu3], axis=1)[:, -CONV_STATE:]
        return y.reshape(nb, nt, D_MODEL), ckv, kpe, new_conv

    def attend_prompt(q, ckv, kpe, k, vt):
        return _flash(q, k, vt, batch, seq)

    y_p, ckv_p, kpe_p, cv_p = stream(
        x_prompt, _rope_tables(jnp.arange(seq)),
        jnp.zeros((1, batch, CONV_STATE, CONV_DIM), F32), 1, False, attend_prompt)

    cache_kr_t = jnp.swapaxes(cache_k_rope, 2, 3)

    def attend_sample(q, ckv, kpe, q_abs):
        o_lat = _paged_attention(
            page_table, q_abs.reshape(db, t_new, N_HEADS * KV_RANK),
            q.reshape(db, t_new, D_HEADS_PAD),
            ckv.reshape(db, t_new, KV_RANK), kpe.reshape(db, t_new, ROPE_DIM),
            cache_kv_latent, cache_kr_t)
        return o_lat.reshape(db * t_new, N_HEADS * KV_RANK)

    pos_s = jnp.tile(past_len + jnp.arange(t_new), TOKEN_TILE // t_new)
    y_s, ckv_s, kpe_s, cv_s = stream(x_sample, _rope_tables(pos_s), state_conv,
                                     CONV_BATCH, True, attend_sample)

    n_pg = seq // PAGE_SIZE
    return (y_p, y_s,
            ckv_p.reshape(1, batch, n_pg, PAGE_SIZE, KV_RANK),
            jnp.swapaxes(kpe_p.reshape(1, batch, n_pg, ROPE_DIM, PAGE_SIZE), 3, 4),
            cv_p[None],
            ckv_s.reshape(1, db, t_new, KV_RANK),
            kpe_s.reshape(1, db, t_new, ROPE_DIM),
            cv_s[None])
```
